```python
import jax, jax.numpy as jnp
from jax import lax
import numpy as np

D_MODEL = 2048
BATCH = 2
SEQ = 8192
DEPTH = 2
DEC_BATCH = 16
DEC_SEQ = 32
PAST_LEN = 2048

CHUNK = 64
N_EVEN = (DEPTH + 1) // 2
N_ODD = DEPTH // 2
EPS = 1e-6
A_WIDTH = D_MODEL // 2
A_GROUPS = 8
A_GROUP_DIM = A_WIDTH // A_GROUPS
A_BLOCK = 128
B_WIDTH = D_MODEL // 2
POOL_WINDOWS = (2, 4, 8, 16)
B_GROUPS = len(POOL_WINDOWS)
B_GROUP_DIM = B_WIDTH // B_GROUPS
POOL_PAD = max(POOL_WINDOWS) - 1
RET_HEADS = 8
RET_QK_DIM = D_MODEL // RET_HEADS
RET_VW = 2 * D_MODEL
RET_V_DIM = RET_VW // RET_HEADS
ROPE_BASE = 10000.0
D_FF = -(-8 * D_MODEL // (3 * 256)) * 256

kernel_name = 'streaming_gmlp_pool_retention_step'


def _rmsnorm(x, g):
    xf = x.astype(jnp.float32)
    y = xf * lax.rsqrt(jnp.mean(xf * xf, axis=-1, keepdims=True) + EPS)
    return (y * g.astype(jnp.float32)).astype(x.dtype)


def _layernorm(x, g, b):
    xf = x.astype(jnp.float32)
    mu = jnp.mean(xf, axis=-1, keepdims=True)
    xc = xf - mu
    y = xc * lax.rsqrt(jnp.mean(xc * xc, axis=-1, keepdims=True) + EPS)
    return (y * g.astype(jnp.float32) + b.astype(jnp.float32)).astype(x.dtype)


def _ada(c, w, b):
    m = jax.nn.silu(c) @ w + b
    return [t[:, None, :] for t in jnp.split(m, 6, axis=-1)]


def _modulate(h, shift, scale):
    return h * (1 + scale) + shift


def _spatial_gate(u, v, w_s, b_s):
    Bn, S, _ = v.shape
    blk = min(S, A_BLOCK)
    nb = S // blk
    idx = jnp.arange(blk)
    mask = (idx[None, :] // CHUNK) <= (idx[:, None] // CHUNK)
    ws = jnp.where(mask[None], w_s[:, :blk, :blk], 0.0).astype(v.dtype)
    vb = v.reshape(Bn, nb, blk, A_GROUPS, A_GROUP_DIM)
    mixed = jnp.einsum('gts,bnsgc->bntgc', ws, vb) + b_s[:, :blk].T[None, None, :, :, None]
    return u * mixed.reshape(Bn, S, A_WIDTH)


def _pool_mix(xb_ext, start, w_pool, pool_scale):
    L = xb_ext.shape[1] - POOL_PAD
    xf = xb_ext.astype(jnp.float32)
    cs = jnp.cumsum(xf, axis=1)
    cs = jnp.concatenate([jnp.zeros_like(cs[:, :1]), cs], axis=1)
    pos = start + jnp.arange(L)
    outs = []
    for g, w in enumerate(POOL_WINDOWS):
        sl = slice(g * B_GROUP_DIM, (g + 1) * B_GROUP_DIM)
        hi = cs[:, POOL_PAD + 1:POOL_PAD + 1 + L, sl]
        lo = cs[:, POOL_PAD + 1 - w:POOL_PAD + 1 - w + L, sl]
        cnt = jnp.minimum(w, pos + 1).astype(jnp.float32)[None, :, None]
        outs.append((hi - lo) / cnt - xf[:, POOL_PAD:, sl])
    pooled = jnp.stack(outs, axis=2)
    y = jnp.einsum('blgc,gcd->blgd', pooled, w_pool.astype(jnp.float32))
    y = y.reshape(xb_ext.shape[0], L, B_WIDTH) * pool_scale.astype(jnp.float32)
    return y.astype(xb_ext.dtype)


def _even_mixer(h, pool_hist, start, w_in, ln_g, ln_b, w_s, b_s, w_pool, pool_scale, w_out):
    proj = h @ w_in
    u, v, xb = jnp.split(proj, [A_WIDTH, 2 * A_WIDTH], axis=-1)
    u = jax.nn.gelu(u)
    v = _layernorm(jax.nn.gelu(v), ln_g, ln_b)
    ya = _spatial_gate(u, v, w_s, b_s)
    xb_ext = jnp.concatenate([pool_hist.astype(xb.dtype), xb], axis=1)
    yb = _pool_mix(xb_ext, start, w_pool, pool_scale)
    y = jnp.concatenate([ya, yb], axis=-1) @ w_out
    return y, xb_ext[:, -POOL_PAD:], v


def _rope(x, pos):
    half = RET_QK_DIM // 2
    freq = ROPE_BASE ** (-jnp.arange(half, dtype=jnp.float32) / half)
    ang = pos.astype(jnp.float32)[:, None] * freq[None, :]
    cos = jnp.cos(ang)[None, :, None, :]
    sin = jnp.sin(ang)[None, :, None, :]
    x1, x2 = x[..., :half], x[..., half:]
    return jnp.concatenate([x1 * cos - x2 * sin, x1 * sin + x2 * cos], axis=-1)


def _retention_block(s_prev, q, k, v, log_g):
    L = q.shape[1]
    n = jnp.arange(L, dtype=jnp.float32)
    diff = n[:, None] - n[None, :]
    decay = jnp.where(diff >= 0, jnp.exp(jnp.maximum(diff, 0.0)[None] * log_g[:, None, None]), 0.0)
    scores = jnp.einsum('blhd,bmhd->bhlm', q, k) * decay[None]
    intra = jnp.einsum('bhlm,bmhe->blhe', scores, v)
    q_dec = q * jnp.exp((n[:, None] + 1.0) * log_g[None, :])[None, :, :, None]
    cross = jnp.einsum('blhd,bhde->blhe', q_dec, s_prev)
    k_dec = k * jnp.exp((L - 1.0 - n)[:, None] * log_g[None, :])[None, :, :, None]
    s_new = s_prev * jnp.exp(L * log_g)[None, :, None, None] + jnp.einsum('blhd,blhe->bhde', k_dec, v)
    return s_new, intra + cross


def _retention(q, k, v, s0, log_g):
    Bn, S = q.shape[:2]
    blk = min(S, CHUNK)
    nb = S // blk

    def to_blocks(t):
        return jnp.moveaxis(t.reshape(Bn, nb, blk, *t.shape[2:]), 1, 0)

    def step(s, qkv):
        return _retention_block(s, qkv[0], qkv[1], qkv[2], log_g)

    s_fin, o = lax.scan(step, s0, (to_blocks(q), to_blocks(k), to_blocks(v)))
    o = jnp.moveaxis(o, 0, 1).reshape(Bn, S, RET_HEADS, RET_V_DIM)
    return o, s_fin


def _odd_mixer(h, s0, start, w_in, w_out):
    Bn, S, _ = h.shape
    proj = h @ w_in
    q, k, v, g = jnp.split(proj, [D_MODEL, 2 * D_MODEL, 2 * D_MODEL + RET_VW], axis=-1)
    q = q.astype(jnp.float32).reshape(Bn, S, RET_HEADS, RET_QK_DIM)
    k = k.astype(jnp.float32).reshape(Bn, S, RET_HEADS, RET_QK_DIM) * (RET_QK_DIM ** -0.5)
    v = v.astype(jnp.float32).reshape(Bn, S, RET_HEADS, RET_V_DIM)
    pos = start + jnp.arange(S)
    q = _rope(q, pos)
    k = _rope(k, pos)
    log_g = jnp.log1p(-jnp.exp2(-5.0 - jnp.arange(RET_HEADS, dtype=jnp.float32)))
    o, s_fin = _retention(q, k, v, s0.astype(jnp.float32), log_g)
    o = o * lax.rsqrt(jnp.mean(o * o, axis=-1, keepdims=True) + EPS)
    o = o.reshape(Bn, S, RET_VW).astype(h.dtype)
    y = (jax.nn.silu(g) * o) @ w_out
    return y, s_fin.astype(h.dtype)


def _swiglu(h, w_gu, w_down):
    gate, up = jnp.split(h @ w_gu, 2, axis=-1)
    return (jax.nn.silu(gate) * up) @ w_down


def setup_inputs(seed: int = 0) -> dict:
    key = jax.random.key(seed)
    ks = jax.random.split(key, 24)
    nrm = jax.random.normal
    f32 = jnp.float32
    D = D_MODEL
    return {
        'x_prompt': nrm(ks[0], (BATCH, SEQ, D), f32),
        'x_sample': nrm(ks[1], (DEC_BATCH, DEC_SEQ, D), f32),
        'c_prompt': nrm(ks[2], (BATCH, D), f32),
        'c_sample': nrm(ks[3], (DEC_BATCH, D), f32),
        'state_b_pool': nrm(ks[4], (N_EVEN, DEC_BATCH, POOL_PAD, B_WIDTH), f32),
        'state_c_ret': 0.5 * nrm(ks[5], (N_ODD, DEC_BATCH, RET_HEADS, RET_QK_DIM, RET_V_DIM), f32),
        'w_ada': 0.5 * D ** -0.5 * nrm(ks[6], (DEPTH, D, 6 * D), f32),
        'b_ada': 0.01 * nrm(ks[7], (DEPTH, 6 * D), f32),
        'norm_mix': 1.0 + 0.1 * nrm(ks[8], (DEPTH, D), f32),
        'norm_ffn': 1.0 + 0.1 * nrm(ks[9], (DEPTH, D), f32),
        'norm_final': 1.0 + 0.1 * nrm(ks[10], (D,), f32),
        'w_in_ab': D ** -0.5 * nrm(ks[11], (N_EVEN, D, 3 * A_WIDTH), f32),
        'ln_v_g': 1.0 + 0.1 * nrm(ks[12], (N_EVEN, A_WIDTH), f32),
        'ln_v_b': 0.01 * nrm(ks[13], (N_EVEN, A_WIDTH), f32),
        'w_s': 0.5 * A_BLOCK ** -0.5 * nrm(ks[14], (N_EVEN, A_GROUPS, A_BLOCK, A_BLOCK), f32),
        'b_s': 1.0 + 0.1 * nrm(ks[15], (N_EVEN, A_GROUPS, A_BLOCK), f32),
        'w_pool': B_GROUP_DIM ** -0.5 * nrm(ks[16], (N_EVEN, B_GROUPS, B_GROUP_DIM, B_GROUP_DIM), f32),
        'pool_scale': 1.0 + 0.1 * nrm(ks[17], (N_EVEN, B_WIDTH), f32),
        'w_out_ab': (A_WIDTH + B_WIDTH) ** -0.5 * nrm(ks[18], (N_EVEN, A_WIDTH + B_WIDTH, D), f32),
        'w_in_c': D ** -0.5 * nrm(ks[19], (N_ODD, D, 2 * D + 2 * RET_VW), f32),
        'w_out_c': RET_VW ** -0.5 * nrm(ks[20], (N_ODD, RET_VW, D), f32),
        'w_ffn_gu': D ** -0.5 * nrm(ks[21], (DEPTH, D, 2 * D_FF), f32),
        'w_ffn_down': D_FF ** -0.5 * nrm(ks[22], (DEPTH, D_FF, D), f32),
    }


def reference(x_prompt, x_sample, c_prompt, c_sample, state_b_pool, state_c_ret,
              w_ada, b_ada, norm_mix, norm_ffn, norm_final,
              w_in_ab, ln_v_g, ln_v_b, w_s, b_s, w_pool, pool_scale, w_out_ab,
              w_in_c, w_out_c, w_ffn_gu, w_ffn_down):
    bp = x_prompt.shape[0]
    zero_pool = jnp.zeros((bp, POOL_PAD, B_WIDTH), x_prompt.dtype)
    zero_ret = jnp.zeros((bp, RET_HEADS, RET_QK_DIM, RET_V_DIM), jnp.float32)
    xp, xs = x_prompt, x_sample
    pool_p, pool_s, v_s, ret_p, ret_s = [], [], [], [], []
    for layer in range(DEPTH):
        mp = _ada(c_prompt, w_ada[layer], b_ada[layer])
        ms = _ada(c_sample, w_ada[layer], b_ada[layer])
        hp = _modulate(_rmsnorm(xp, norm_mix[layer]), mp[0], mp[1])
        hs = _modulate(_rmsnorm(xs, norm_mix[layer]), ms[0], ms[1])
        if layer % 2 == 0:
            e = layer // 2
            prm = (w_in_ab[e], ln_v_g[e], ln_v_b[e], w_s[e], b_s[e], w_pool[e], pool_scale[e], w_out_ab[e])
            op, new_pp, _ = _even_mixer(hp, zero_pool, 0, *prm)
            osm, new_ps, new_vs = _even_mixer(hs, state_b_pool[e], PAST_LEN, *prm)
            pool_p.append(new_pp)
            pool_s.append(new_ps)
            v_s.append(new_vs)
        else:
            o = layer // 2
            op, new_rp = _odd_mixer(hp, zero_ret, 0, w_in_c[o], w_out_c[o])
            osm, new_rs = _odd_mixer(hs, state_c_ret[o], PAST_LEN, w_in_c[o], w_out_c[o])
            ret_p.append(new_rp)
            ret_s.append(new_rs)
        xp = xp + mp[2] * op
        xs = xs + ms[2] * osm
        hp = _modulate(_rmsnorm(xp, norm_ffn[layer]), mp[3], mp[4])
        hs = _modulate(_rmsnorm(xs, norm_ffn[layer]), ms[3], ms[4])
        xp = xp + mp[5] * _swiglu(hp, w_ffn_gu[layer], w_ffn_down[layer])
        xs = xs + ms[5] * _swiglu(hs, w_ffn_gu[layer], w_ffn_down[layer])
    y_prompt = _rmsnorm(xp, norm_final)
    y_sample = _rmsnorm(xs, norm_final)
    return (y_prompt, y_sample, jnp.stack(pool_p), jnp.stack(pool_s), jnp.stack(v_s), jnp.stack(ret_p), jnp.stack(ret_s))
```

```python
import functools
import math

import jax
import jax.numpy as jnp
from jax import lax
from jax.experimental import pallas as pl
from jax.experimental.pallas import tpu as pltpu

EPS = 1e-6
CHUNK = 64
A_BLOCK = 128
A_GROUPS = 8
POOL_WINDOWS = (2, 4, 8, 16)
POOL_PAD = 15
POOL_HIST = 16
RET_HEADS = 8
ROPE_BASE = 10000.0

F32 = jnp.float32
BF16 = jnp.bfloat16

VMEM_LIMIT_BYTES = 56 * 1024 * 1024
ROW_TILE = 512
FFN_TILE = 512
PROJ_TILE = 512
RET_CHUNK = 256


def _params(*sem):
    return pltpu.CompilerParams(dimension_semantics=sem, vmem_limit_bytes=VMEM_LIMIT_BYTES)


def _resident(shape):
    nd = len(shape)
    return pl.BlockSpec(shape, lambda *_: (0,) * nd, pipeline_mode=pl.Buffered(1))


def _dot(a, b):
    return jnp.dot(a, b, preferred_element_type=F32)


def _silu(x):
    return x * jax.nn.sigmoid(x)


def _norm_modulate(x_ref, g_ref, mod_ref, i_shift, i_scale):
    x = x_ref[...]
    gn, rn, dn = x.shape
    ms = jnp.mean(x * x, axis=-1, keepdims=True)
    y = x * lax.rsqrt(ms + EPS) * g_ref[...][None]
    shift = mod_ref[:, i_shift:i_shift + 1, :]
    scale = mod_ref[:, i_scale:i_scale + 1, :]
    h = y * (1.0 + scale) + shift
    return h.reshape(gn * rn, dn)


def _ada_kernel(c_ref, w_ref, b_ref, o_ref):
    s = _silu(c_ref[...]).astype(BF16)
    o_ref[0] = _dot(s, w_ref[0].astype(BF16)) + b_ref[0]


def _ada(c_all, w_ada, b_ada):
    depth, d, n = w_ada.shape
    rows = c_all.shape[0]
    tn = 1024
    return pl.pallas_call(
        _ada_kernel,
        grid=(depth, n // tn),
        in_specs=[
            pl.BlockSpec((rows, d), lambda l, j: (0, 0)),
            pl.BlockSpec((1, d, tn), lambda l, j: (l, 0, j)),
            pl.BlockSpec((1, 1, tn), lambda l, j: (l, 0, j)),
        ],
        out_specs=pl.BlockSpec((1, rows, tn), lambda l, j: (l, 0, j)),
        out_shape=jax.ShapeDtypeStruct((depth, rows, n), F32),
        compiler_params=_params("arbitrary", "arbitrary"),
        name="ada",
    )(c_all, w_ada, b_ada.reshape(depth, 1, n))


def _even_in_kernel(x_ref, g_ref, mod_ref, w_ref, lng_ref, lnb_ref, u_ref, v_ref, xb_ref):
    aw = u_ref.shape[-1]
    h = _norm_modulate(x_ref, g_ref, mod_ref, 0, 1).astype(BF16)
    u = _dot(h, w_ref[:, 0:aw])
    u_ref[...] = jax.nn.gelu(u).astype(u_ref.dtype)
    v = jax.nn.gelu(_dot(h, w_ref[:, aw:2 * aw]))
    mu = jnp.mean(v, axis=-1, keepdims=True)
    vc = v - mu
    vn = vc * lax.rsqrt(jnp.mean(vc * vc, axis=-1, keepdims=True) + EPS)
    v_ref[...] = (vn * lng_ref[...] + lnb_ref[...]).astype(v_ref.dtype)
    xb_ref[...] = _dot(h, w_ref[:, 2 * aw:3 * aw])


def _even_in(x3, g, mod, w_in, ln_g, ln_b, groups_per_tile, batch_of_tile, uv_dtype):
    ng, rn, d = x3.shape
    gt = groups_per_tile
    tm = gt * rn
    m = ng * rn
    aw = w_in.shape[1] // 3
    row = lambda i: (i, 0)
    return pl.pallas_call(
        _even_in_kernel,
        grid=(ng // gt,),
        in_specs=[
            pl.BlockSpec((gt, rn, d), lambda i: (i, 0, 0)),
            _resident((1, d)),
            pl.BlockSpec((gt, 6, d), lambda i: (batch_of_tile(i), 0, 0)),
            _resident(w_in.shape),
            _resident((1, aw)),
            _resident((1, aw)),
        ],
        out_specs=[pl.BlockSpec((tm, aw), row)] * 3,
        out_shape=[
            jax.ShapeDtypeStruct((m, aw), uv_dtype),
            jax.ShapeDtypeStruct((m, aw), uv_dtype),
            jax.ShapeDtypeStruct((m, aw), F32),
        ],
        compiler_params=_params("arbitrary"),
        name="even_in",
    )(x3, g, mod, w_in, ln_g, ln_b)


def _even_mix_kernel(u_ref, v_ref, xb_ref, hist_ref, x_ref, mod_ref, ws_ref, bs_ref, wp_ref,
                     ps_ref, wo_ref, o_ref, y_ref, *, start, tiles_per_seq, zero_first_hist):
    i = pl.program_id(0)
    tm, aw = u_ref.shape
    nseg, seg, bw = xb_ref.shape
    nblk = tm // A_BLOCK
    gdim = aw // A_GROUPS

    for g in range(A_GROUPS):
        cols = slice(g * gdim, (g + 1) * gdim)
        vg = jnp.concatenate(
            [v_ref[n * A_BLOCK:(n + 1) * A_BLOCK, cols].astype(BF16) for n in range(nblk)], axis=1)
        mixed = _dot(ws_ref[g], vg) + bs_ref[:, g:g + 1]
        for n in range(nblk):
            rows = slice(n * A_BLOCK, (n + 1) * A_BLOCK)
            piece = mixed[:, n * gdim:(n + 1) * gdim]
            y_ref[rows, cols] = (u_ref[rows, cols].astype(F32) * piece).astype(BF16)

    hist = hist_ref[...]
    if zero_first_hist:
        hist = jnp.where(i % tiles_per_seq == 0, 0.0, hist)
    ext = jnp.concatenate([hist, xb_ref[...]], axis=1)
    off = start + (i % tiles_per_seq) * seg if tiles_per_seq > 1 else start
    posp1 = lax.broadcasted_iota(jnp.int32, (1, seg, 1), 1) + (off + 1)
    pdim = bw // len(POOL_WINDOWS)
    for g, w in enumerate(POOL_WINDOWS):
        cols = slice(g * pdim, (g + 1) * pdim)
        a = ext[:, :, cols]
        tot = a
        k = 1
        while k < w:
            tot = tot + pltpu.roll(tot, k, axis=1)
            k *= 2
        inv = 1.0 / jnp.minimum(posp1, w).astype(F32)
        pooled = tot[:, POOL_HIST:, :] * inv - a[:, POOL_HIST:, :]
        pooled = pooled.reshape(tm, pdim).astype(BF16)
        yb = _dot(pooled, wp_ref[g]) * ps_ref[:, cols]
        y_ref[:, aw + g * pdim:aw + (g + 1) * pdim] = yb.astype(BF16)

    out = _dot(y_ref[...], wo_ref[...])
    x = x_ref[...]
    gate = mod_ref[:, 2:3, :]
    o_ref[...] = x + gate * out.reshape(x.shape)


def _even_mix(u, v, xb3, hist, x3, mod, ws, bs_t, w_pool, pool_scale, w_out, groups_per_tile,
              batch_of_tile, hist_of_tile, start, tiles_per_seq, zero_first_hist):
    ng, rn, d = x3.shape
    gt = groups_per_tile
    tm = gt * rn
    m, aw = u.shape
    nseg_total, seg, bw = xb3.shape
    nseg = tm // seg
    kern = functools.partial(_even_mix_kernel, start=start, tiles_per_seq=tiles_per_seq,
                             zero_first_hist=zero_first_hist)
    return pl.pallas_call(
        kern,
        grid=(m // tm,),
        in_specs=[
            pl.BlockSpec((tm, aw), lambda i: (i, 0)),
            pl.BlockSpec((tm, aw), lambda i: (i, 0)),
            pl.BlockSpec((nseg, seg, bw), lambda i: (i, 0, 0)),
            pl.BlockSpec((nseg, POOL_HIST, bw), lambda i: (hist_of_tile(i), 0, 0)),
            pl.BlockSpec((gt, rn, d), lambda i: (i, 0, 0)),
            pl.BlockSpec((gt, 6, d), lambda i: (batch_of_tile(i), 0, 0)),
            _resident(ws.shape),
            _resident(bs_t.shape),
            _resident(w_pool.shape),
            _resident(pool_scale.shape),
            _resident(w_out.shape),
        ],
        out_specs=pl.BlockSpec((gt, rn, d), lambda i: (i, 0, 0)),
        out_shape=jax.ShapeDtypeStruct(x3.shape, F32),
        scratch_shapes=[pltpu.VMEM((tm, aw + bw), BF16)],
        compiler_params=_params("arbitrary"),
        name="even_mix",
    )(u, v, xb3, hist, x3, mod, ws, bs_t, w_pool, pool_scale, w_out)


def _ffn_kernel(x_ref, g_ref, mod_ref, wg_ref, wu_ref, wd_ref, gf_ref, o_ref, h_ref, acc_ref,
                *, final_norm):
    j = pl.program_id(1)

    @pl.when(j == 0)
    def _():
        h_ref[...] = _norm_modulate(x_ref, g_ref, mod_ref, 3, 4).astype(BF16)
        acc_ref[...] = jnp.zeros_like(acc_ref)

    h = h_ref[...]
    a = (_silu(_dot(h, wg_ref[...])) * _dot(h, wu_ref[...])).astype(BF16)
    acc_ref[...] += _dot(a, wd_ref[...])

    @pl.when(j == pl.num_programs(1) - 1)
    def _():
        x = x_ref[...]
        y = x + mod_ref[:, 5:6, :] * acc_ref[...].reshape(x.shape)
        if final_norm:
            ms = jnp.mean(y * y, axis=-1, keepdims=True)
            y = y * lax.rsqrt(ms + EPS) * gf_ref[...][None]
        o_ref[...] = y


def _ffn(x3, g, mod, w_gu, w_down, g_final, groups_per_tile, batch_of_tile, final_norm):
    ng, rn, d = x3.shape
    gt = groups_per_tile
    tm = gt * rn
    dff = w_down.shape[0]
    tf = FFN_TILE
    nf = dff // tf
    return pl.pallas_call(
        functools.partial(_ffn_kernel, final_norm=final_norm),
        grid=(ng // gt, nf),
        in_specs=[
            pl.BlockSpec((gt, rn, d), lambda i, j: (i, 0, 0)),
            _resident((1, d)),
            pl.BlockSpec((gt, 6, d), lambda i, j: (batch_of_tile(i), 0, 0)),
            pl.BlockSpec((d, tf), lambda i, j: (0, j)),
            pl.BlockSpec((d, tf), lambda i, j: (0, j + nf)),
            pl.BlockSpec((tf, d), lambda i, j: (j, 0)),
            _resident((1, d)),
        ],
        out_specs=pl.BlockSpec((gt, rn, d), lambda i, j: (i, 0, 0)),
        out_shape=jax.ShapeDtypeStruct(x3.shape, F32),
        scratch_shapes=[pltpu.VMEM((tm, d), BF16), pltpu.VMEM((tm, d), F32)],
        compiler_params=_params("arbitrary", "arbitrary"),
        name="ffn",
    )(x3, g, mod, w_gu, w_gu, w_down, g_final)


def _odd_in_kernel(x_ref, g_ref, mod_ref, w_ref, o_ref, h_ref, cos_ref, sin_ref,
                   *, start, tiles_per_seq, qk_cols, v_cols, head_dim):
    i = pl.program_id(0)
    j = pl.program_id(1)
    tm = h_ref.shape[0]
    tn = o_ref.shape[1]
    half = head_dim // 2

    @pl.when(j == 0)
    def _():
        h_ref[...] = _norm_modulate(x_ref, g_ref, mod_ref, 0, 1).astype(BF16)
        gn, rn, _ = x_ref.shape
        row = lax.broadcasted_iota(jnp.int32, (gn, rn, half), 1).reshape(tm, half)
        pos = (row + (start + (i % tiles_per_seq) * rn)).astype(F32)
        lane = lax.broadcasted_iota(jnp.int32, (tm, half), 1).astype(F32)
        ang = pos * jnp.exp(lane * (-math.log(ROPE_BASE) / half))
        cos_ref[...] = jnp.cos(ang)
        sin_ref[...] = jnp.sin(ang)

    acc = _dot(h_ref[...], w_ref[...])
    nq = qk_cols // tn
    nv = v_cols // tn

    def rope(scale):
        cos = cos_ref[...]
        sin = sin_ref[...]
        for hd in range(tn // head_dim):
            x1 = acc[:, hd * head_dim:hd * head_dim + half]
            x2 = acc[:, hd * head_dim + half:(hd + 1) * head_dim]
            o_ref[:, hd * head_dim:hd * head_dim + half] = ((x1 * cos - x2 * sin) * scale).astype(BF16)
            o_ref[:, hd * head_dim + half:(hd + 1) * head_dim] = ((x1 * sin + x2 * cos) * scale).astype(BF16)

    @pl.when(j < nq)
    def _():
        rope(1.0)

    @pl.when(jnp.logical_and(j >= nq, j < 2 * nq))
    def _():
        rope(head_dim ** -0.5)

    @pl.when(jnp.logical_and(j >= 2 * nq, j < 2 * nq + nv))
    def _():
        o_ref[...] = acc.astype(BF16)

    @pl.when(j >= 2 * nq + nv)
    def _():
        o_ref[...] = _silu(acc).astype(BF16)


def _odd_in(x3, g, mod, w_in, groups_per_tile, batch_of_tile, start, tiles_per_seq):
    ng, rn, d = x3.shape
    gt = groups_per_tile
    tm = gt * rn
    m = ng * rn
    n = w_in.shape[1]
    tn = PROJ_TILE
    head_dim = d // RET_HEADS
    kern = functools.partial(_odd_in_kernel, start=start, tiles_per_seq=tiles_per_seq, qk_cols=d,
                             v_cols=2 * d, head_dim=head_dim)
    return pl.pallas_call(
        kern,
        grid=(ng // gt, n // tn),
        in_specs=[
            pl.BlockSpec((gt, rn, d), lambda i, j: (i, 0, 0)),
            _resident((1, d)),
            pl.BlockSpec((gt, 6, d), lambda i, j: (batch_of_tile(i), 0, 0)),
            pl.BlockSpec((d, tn), lambda i, j: (0, j)),
        ],
        out_specs=pl.BlockSpec((tm, tn), lambda i, j: (i, j)),
        out_shape=jax.ShapeDtypeStruct((m, n), BF16),
        scratch_shapes=[pltpu.VMEM((tm, d), BF16), pltpu.VMEM((tm, head_dim // 2), F32),
                        pltpu.VMEM((tm, head_dim // 2), F32)],
        compiler_params=_params("arbitrary", "arbitrary"),
        name="odd_in",
    )(x3, g, mod, w_in)


def _retention_kernel(lg_ref, q_ref, k_ref, v_ref, g_ref, s0_ref, o_ref, sf_ref,
                      s_ref, dec_ref, dq_ref, dk_ref, *, zero_state):
    hd = pl.program_id(1)
    c = pl.program_id(2)
    cl, dk_dim = q_ref.shape
    dv_dim = v_ref.shape[1]
    log_g = lg_ref[hd]

    @pl.when(c == 0)
    def _():
        if zero_state:
            s_ref[...] = jnp.zeros_like(s_ref)
        else:
            s_ref[...] = s0_ref[0, 0]
        li = lax.broadcasted_iota(jnp.int32, (cl, cl), 0)
        mi = lax.broadcasted_iota(jnp.int32, (cl, cl), 1)
        diff = li - mi
        dec = jnp.exp(jnp.maximum(diff, 0).astype(F32) * log_g)
        dec_ref[...] = jnp.where(diff >= 0, dec, 0.0)
        rq = lax.broadcasted_iota(jnp.int32, (cl, dv_dim), 0).astype(F32)
        dq_ref[...] = jnp.exp((rq + 1.0) * log_g)
        rk = lax.broadcasted_iota(jnp.int32, (cl, dk_dim), 0).astype(F32)
        dk_ref[...] = jnp.exp((cl - 1.0 - rk) * log_g)

    q = q_ref[...]
    k = k_ref[...]
    v = v_ref[...]
    s_prev = s_ref[...]
    scores = lax.dot_general(q, k, (((1,), (1,)), ((), ())), preferred_element_type=F32)
    p = (scores * dec_ref[...]).astype(BF16)
    o = _dot(p, v) + _dot(q, s_prev.astype(BF16)) * dq_ref[...]
    k_dec = (k.astype(F32) * dk_ref[...]).astype(BF16)
    kv = lax.dot_general(k_dec, v, (((0,), (0,)), ((), ())), preferred_element_type=F32)
    s_new = s_prev * jnp.exp(jnp.full((1, 1), cl, F32) * log_g) + kv
    s_ref[...] = s_new

    o = o * lax.rsqrt(jnp.mean(o * o, axis=-1, keepdims=True) + EPS)
    o_ref[...] = (g_ref[...].astype(F32) * o).astype(BF16)

    @pl.when(c == pl.num_programs(2) - 1)
    def _():
        sf_ref[0, 0] = s_new


def _retention(proj, s0, log_g, batch, seq_len, d_model, zero_state):
    m = proj.shape[0]
    hds = RET_HEADS
    dk_dim = d_model // hds
    dv_dim = 2 * d_model // hds
    cl = min(seq_len, RET_CHUNK)
    nc = seq_len // cl
    kq = d_model // dk_dim
    kv0 = 2 * d_model // dv_dim
    kg0 = kv0 + 2 * d_model // dv_dim
    state_spec = pl.BlockSpec((1, 1, dk_dim, dv_dim), lambda b, h, c: (b, h, 0, 0))
    s0_spec = _resident(s0.shape) if zero_state else state_spec
    return pl.pallas_call(
        functools.partial(_retention_kernel, zero_state=zero_state),
        grid=(batch, hds, nc),
        in_specs=[
            pl.BlockSpec(memory_space=pltpu.SMEM),
            pl.BlockSpec((cl, dk_dim), lambda b, h, c: (b * nc + c, h)),
            pl.BlockSpec((cl, dk_dim), lambda b, h, c: (b * nc + c, kq + h)),
            pl.BlockSpec((cl, dv_dim), lambda b, h, c: (b * nc + c, kv0 + h)),
            pl.BlockSpec((cl, dv_dim), lambda b, h, c: (b * nc + c, kg0 + h)),
            s0_spec,
        ],
        out_specs=[
            pl.BlockSpec((cl, dv_dim), lambda b, h, c: (b * nc + c, h)),
            state_spec,
        ],
        out_shape=[
            jax.ShapeDtypeStruct((m, 2 * d_model), BF16),
            jax.ShapeDtypeStruct((batch, hds, dk_dim, dv_dim), F32),
        ],
        scratch_shapes=[
            pltpu.VMEM((dk_dim, dv_dim), F32),
            pltpu.VMEM((cl, cl), F32),
            pltpu.VMEM((cl, dv_dim), F32),
            pltpu.VMEM((cl, dk_dim), F32),
        ],
        compiler_params=_params("arbitrary", "arbitrary", "arbitrary"),
        name="retention",
    )(log_g, proj, proj, proj, proj, s0)


def _out_proj_kernel(a_ref, x_ref, mod_ref, w_ref, o_ref):
    x = x_ref[...]
    y = _dot(a_ref[...], w_ref[...])
    o_ref[...] = x + mod_ref[:, 2:3, :] * y.reshape(x.shape)


def _out_proj(a, x3, mod, w, groups_per_tile, batch_of_tile):
    ng, rn, d = x3.shape
    gt = groups_per_tile
    tm = gt * rn
    kdim = a.shape[1]
    return pl.pallas_call(
        _out_proj_kernel,
        grid=(ng // gt,),
        in_specs=[
            pl.BlockSpec((tm, kdim), lambda i: (i, 0)),
            pl.BlockSpec((gt, rn, d), lambda i: (i, 0, 0)),
            pl.BlockSpec((gt, 6, d), lambda i: (batch_of_tile(i), 0, 0)),
            _resident(w.shape),
        ],
        out_specs=pl.BlockSpec((gt, rn, d), lambda i: (i, 0, 0)),
        out_shape=jax.ShapeDtypeStruct(x3.shape, F32),
        compiler_params=_params("arbitrary"),
        name="out_proj",
    )(a, x3, mod, w)


def _stream(x, mods, pool_hist, ret_state, start, prm, uv_dtype):
    batch, seq_len, d = x.shape
    m = batch * seq_len
    if seq_len >= ROW_TILE:
        rn, gt = ROW_TILE, 1
        tiles_per_seq = seq_len // ROW_TILE
        batch_of_tile = lambda i: i // tiles_per_seq
    else:
        rn, gt = seq_len, ROW_TILE // seq_len
        tiles_per_seq = 1
        batch_of_tile = lambda i: i
    tm = rn * gt
    x3 = x.reshape(m // rn, rn, d)
    seg = min(seq_len, tm)

    mod = mods[0]
    u, v, xb = _even_in(x3, prm["norm_mix"][0], mod, prm["w_in_ab"], prm["ln_v_g"], prm["ln_v_b"],
                        gt, batch_of_tile, uv_dtype)
    bw = xb.shape[1]
    xb3 = xb.reshape(m // seg, seg, bw)
    if pool_hist is None:
        hist = xb.reshape(m // POOL_HIST, POOL_HIST, bw)
        per_tile = tm // POOL_HIST
        hist_of_tile = lambda i: jnp.maximum(i * per_tile - 1, 0)
        zero_first = True
    else:
        hist = jnp.pad(pool_hist, ((0, 0), (POOL_HIST - POOL_PAD, 0), (0, 0)))
        hist_of_tile = lambda i: i
        zero_first = False
    x3 = _even_mix(u, v, xb3, hist, x3, mod, prm["ws"][seq_len], prm["bs_t"][seq_len],
                   prm["w_pool"], prm["pool_scale"], prm["w_out_ab"], gt, batch_of_tile,
                   hist_of_tile, start, tiles_per_seq, zero_first)
    x3 = _ffn(x3, prm["norm_ffn"][0], mod, prm["w_ffn_gu"][0], prm["w_ffn_down"][0],
              prm["norm_final"], gt, batch_of_tile, False)
    pool_state = xb.reshape(batch, seq_len, bw)[:, seq_len - POOL_PAD:]
    v_state = v.reshape(batch, seq_len, -1)

    mod = mods[1]
    proj = _odd_in(x3, prm["norm_mix"][1], mod, prm["w_in_c"], gt, batch_of_tile, start, tiles_per_seq)
    if ret_state is None:
        s0 = jnp.zeros((1, 1, d // RET_HEADS, 2 * d // RET_HEADS), F32)
    else:
        s0 = ret_state
    o, s_fin = _retention(proj, s0, prm["log_g"], batch, seq_len, d, ret_state is None)
    x3 = _out_proj(o, x3, mod, prm["w_out_c"], gt, batch_of_tile)
    y3 = _ffn(x3, prm["norm_ffn"][1], mod, prm["w_ffn_gu"][1], prm["w_ffn_down"][1],
              prm["norm_final"], gt, batch_of_tile, True)
    return y3.reshape(batch, seq_len, d), pool_state, v_state, s_fin


def _spatial_weights(w_s, b_s, seq_len):
    blk = min(seq_len, A_BLOCK)
    idx = jnp.arange(A_BLOCK)
    t, s = idx[:, None], idx[None, :]
    same_seq = (t // blk) == (s // blk)
    causal = ((s % blk) // CHUNK) <= ((t % blk) // CHUNK)
    ws = jnp.where((same_seq & causal)[None], w_s[:, t % blk, s % blk], 0.0).astype(BF16)
    bs_t = b_s[:, idx % blk].T
    return ws, bs_t


def kernel(x_prompt, x_sample, c_prompt, c_sample, state_b_pool, state_c_ret, w_ada, b_ada, norm_mix, norm_ffn, norm_final, w_in_ab, ln_v_g, ln_v_b, w_s, b_s, w_pool, pool_scale, w_out_ab, w_in_c, w_out_c, w_ffn_gu, w_ffn_down):
    bp, sp, d = x_prompt.shape
    bs, ss, _ = x_sample.shape
    depth = w_ada.shape[0]
    past_len = 2048

    rows = bp + bs
    pad_rows = -rows % 16
    c_all = jnp.concatenate([c_prompt, c_sample, jnp.zeros((pad_rows, d), F32)], axis=0)
    ada = _ada(c_all, w_ada, b_ada).reshape(depth, rows + pad_rows, 6, d)
    mods_p = [ada[l, :bp] for l in range(depth)]
    mods_s = [ada[l, bp:rows] for l in range(depth)]

    log_g = jnp.log1p(-jnp.exp2(-5.0 - jnp.arange(RET_HEADS, dtype=F32)))
    prm = {
        "norm_mix": norm_mix.reshape(depth, 1, d),
        "norm_ffn": norm_ffn.reshape(depth, 1, d),
        "norm_final": norm_final.reshape(1, d),
        "w_in_ab": w_in_ab[0].astype(BF16),
        "ln_v_g": ln_v_g[0].reshape(1, -1),
        "ln_v_b": ln_v_b[0].reshape(1, -1),
        "ws": {}, "bs_t": {},
        "w_pool": w_pool[0].astype(BF16),
        "pool_scale": pool_scale[0].reshape(1, -1),
        "w_out_ab": w_out_ab[0].astype(BF16),
        "w_in_c": w_in_c[0].astype(BF16),
        "w_out_c": w_out_c[0].astype(BF16),
        "w_ffn_gu": w_ffn_gu.astype(BF16),
        "w_ffn_down": w_ffn_down.astype(BF16),
        "log_g": log_g,
    }
    for sl in (sp, ss):
        prm["ws"][sl], prm["bs_t"][sl] = _spatial_weights(w_s[0], b_s[0], sl)

    y_p, pool_p, _, ret_p = _stream(x_prompt, mods_p, None, None, 0, prm, BF16)
    y_s, pool_s, v_s, ret_s = _stream(x_sample, mods_s, state_b_pool[0], state_c_ret[0], past_len,
                                      prm, F32)
    return (y_p, y_s, pool_p[None], pool_s[None], v_s[None], ret_p[None], ret_s[None])
```

```python
import functools
import math

import jax
import jax.numpy as jnp
from jax import lax
from jax.experimental import pallas as pl
from jax.experimental.pallas import tpu as pltpu

EPS = 1e-6
CHUNK = 64
A_BLOCK = 128
A_GROUPS = 8
POOL_WINDOWS = (2, 4, 8, 16)
POOL_PAD = 15
POOL_HIST = 16
RET_HEADS = 8
ROPE_BASE = 10000.0
LANES = 128

F32 = jnp.float32
BF16 = jnp.bfloat16

VMEM_LIMIT_BYTES = 56 * 1024 * 1024
ROW_TILE = 512
FFN_TILE = 512
PROJ_TILE = 2048
PROJ_SUB = 512
RET_HEAD_BLOCK = 8
RET_CHUNK = 256


def _params(*sem):
    return pltpu.CompilerParams(dimension_semantics=sem, vmem_limit_bytes=VMEM_LIMIT_BYTES)


def _resident(shape):
    nd = len(shape)
    return pl.BlockSpec(shape, lambda *_: (0,) * nd, pipeline_mode=pl.Buffered(1))


def _dot(a, b):
    return jnp.dot(a, b, preferred_element_type=F32)


def _silu(x):
    return x * jax.nn.sigmoid(x)


def _norm_modulate(x_ref, g_ref, mod_ref, i_shift, i_scale):
    x = x_ref[...]
    gn, rn, dn = x.shape
    ms = jnp.mean(x * x, axis=-1, keepdims=True)
    y = x * lax.rsqrt(ms + EPS) * g_ref[...][None]
    shift = mod_ref[:, i_shift:i_shift + 1, :]
    scale = mod_ref[:, i_scale:i_scale + 1, :]
    h = y * (1.0 + scale) + shift
    return h.reshape(gn * rn, dn)


def _ada_kernel(c_ref, w_ref, b_ref, o_ref):
    s = _silu(c_ref[...]).astype(BF16)
    o_ref[0] = _dot(s, w_ref[0].astype(BF16)) + b_ref[0]


def _ada(c_all, w_ada, b_ada):
    depth, d, n = w_ada.shape
    rows = c_all.shape[0]
    tn = 1024
    return pl.pallas_call(
        _ada_kernel,
        grid=(depth, n // tn),
        in_specs=[
            pl.BlockSpec((rows, d), lambda l, j: (0, 0)),
            pl.BlockSpec((1, d, tn), lambda l, j: (l, 0, j)),
            pl.BlockSpec((1, 1, tn), lambda l, j: (l, 0, j)),
        ],
        out_specs=pl.BlockSpec((1, rows, tn), lambda l, j: (l, 0, j)),
        out_shape=jax.ShapeDtypeStruct((depth, rows, n), F32),
        compiler_params=_params("arbitrary", "arbitrary"),
        name="ada",
    )(c_all, w_ada, b_ada.reshape(depth, 1, n))


def _even_in_kernel(x_ref, g_ref, mod_ref, w_ref, lng_ref, lnb_ref, u_ref, v_ref, xb_ref):
    aw = u_ref.shape[-1]
    h = _norm_modulate(x_ref, g_ref, mod_ref, 0, 1).astype(BF16)
    u = _dot(h, w_ref[:, 0:aw])
    u_ref[...] = jax.nn.gelu(u).astype(u_ref.dtype)
    v = jax.nn.gelu(_dot(h, w_ref[:, aw:2 * aw]))
    mu = jnp.mean(v, axis=-1, keepdims=True)
    vc = v - mu
    vn = vc * lax.rsqrt(jnp.mean(vc * vc, axis=-1, keepdims=True) + EPS)
    v_ref[...] = (vn * lng_ref[...] + lnb_ref[...]).astype(v_ref.dtype)
    xb_ref[...] = _dot(h, w_ref[:, 2 * aw:3 * aw])


def _even_in(x3, g, mod, w_in, ln_g, ln_b, groups_per_tile, batch_of_tile, uv_dtype):
    ng, rn, d = x3.shape
    gt = groups_per_tile
    tm = gt * rn
    m = ng * rn
    aw = w_in.shape[1] // 3
    row = lambda i: (i, 0)
    return pl.pallas_call(
        _even_in_kernel,
        grid=(ng // gt,),
        in_specs=[
            pl.BlockSpec((gt, rn, d), lambda i: (i, 0, 0)),
            _resident((1, d)),
            pl.BlockSpec((gt, 6, d), lambda i: (batch_of_tile(i), 0, 0)),
            _resident(w_in.shape),
            _resident((1, aw)),
            _resident((1, aw)),
        ],
        out_specs=[pl.BlockSpec((tm, aw), row)] * 3,
        out_shape=[
            jax.ShapeDtypeStruct((m, aw), uv_dtype),
            jax.ShapeDtypeStruct((m, aw), uv_dtype),
            jax.ShapeDtypeStruct((m, aw), F32),
        ],
        compiler_params=_params("arbitrary"),
        name="even_in",
    )(x3, g, mod, w_in, ln_g, ln_b)


def _even_mix_kernel(u_ref, v_ref, xb_ref, hist_ref, x_ref, mod_ref, ws_ref, bs_ref, wp_ref,
                     ps_ref, wo_ref, o_ref, y_ref, *, start, tiles_per_seq, zero_first_hist):
    i = pl.program_id(0)
    tm, aw = u_ref.shape
    nseg, seg, bw = xb_ref.shape
    nblk = tm // A_BLOCK
    gdim = aw // A_GROUPS

    for g in range(A_GROUPS):
        cols = slice(g * gdim, (g + 1) * gdim)
        vg = jnp.concatenate(
            [v_ref[n * A_BLOCK:(n + 1) * A_BLOCK, cols].astype(BF16) for n in range(nblk)], axis=1)
        mixed = _dot(ws_ref[g], vg) + bs_ref[:, g:g + 1]
        for n in range(nblk):
            rows = slice(n * A_BLOCK, (n + 1) * A_BLOCK)
            piece = mixed[:, n * gdim:(n + 1) * gdim]
            y_ref[rows, cols] = (u_ref[rows, cols].astype(F32) * piece).astype(BF16)

    hist = hist_ref[...]
    if zero_first_hist:
        hist = jnp.where(i % tiles_per_seq == 0, 0.0, hist)
    ext = jnp.concatenate([hist, xb_ref[...]], axis=1)
    off = start + (i % tiles_per_seq) * seg if tiles_per_seq > 1 else start
    posp1 = lax.broadcasted_iota(jnp.int32, (1, seg, 1), 1) + (off + 1)
    pdim = bw // len(POOL_WINDOWS)
    for g, w in enumerate(POOL_WINDOWS):
        cols = slice(g * pdim, (g + 1) * pdim)
        a = ext[:, :, cols]
        tot = a
        k = 1
        while k < w:
            tot = tot + pltpu.roll(tot, k, axis=1)
            k *= 2
        inv = 1.0 / jnp.minimum(posp1, w).astype(F32)
        pooled = tot[:, POOL_HIST:, :] * inv - a[:, POOL_HIST:, :]
        pooled = pooled.reshape(tm, pdim).astype(BF16)
        yb = _dot(pooled, wp_ref[g]) * ps_ref[:, cols]
        y_ref[:, aw + g * pdim:aw + (g + 1) * pdim] = yb.astype(BF16)

    out = _dot(y_ref[...], wo_ref[...])
    x = x_ref[...]
    gate = mod_ref[:, 2:3, :]
    o_ref[...] = x + gate * out.reshape(x.shape)


def _even_mix(u, v, xb3, hist, x3, mod, ws, bs_t, w_pool, pool_scale, w_out, groups_per_tile,
              batch_of_tile, hist_of_tile, start, tiles_per_seq, zero_first_hist):
    ng, rn, d = x3.shape
    gt = groups_per_tile
    tm = gt * rn
    m, aw = u.shape
    nseg_total, seg, bw = xb3.shape
    nseg = tm // seg
    kern = functools.partial(_even_mix_kernel, start=start, tiles_per_seq=tiles_per_seq,
                             zero_first_hist=zero_first_hist)
    return pl.pallas_call(
        kern,
        grid=(m // tm,),
        in_specs=[
            pl.BlockSpec((tm, aw), lambda i: (i, 0)),
            pl.BlockSpec((tm, aw), lambda i: (i, 0)),
            pl.BlockSpec((nseg, seg, bw), lambda i: (i, 0, 0)),
            pl.BlockSpec((nseg, POOL_HIST, bw), lambda i: (hist_of_tile(i), 0, 0)),
            pl.BlockSpec((gt, rn, d), lambda i: (i, 0, 0)),
            pl.BlockSpec((gt, 6, d), lambda i: (batch_of_tile(i), 0, 0)),
            _resident(ws.shape),
            _resident(bs_t.shape),
            _resident(w_pool.shape),
            _resident(pool_scale.shape),
            _resident(w_out.shape),
        ],
        out_specs=pl.BlockSpec((gt, rn, d), lambda i: (i, 0, 0)),
        out_shape=jax.ShapeDtypeStruct(x3.shape, F32),
        scratch_shapes=[pltpu.VMEM((tm, aw + bw), BF16)],
        compiler_params=_params("arbitrary"),
        name="even_mix",
    )(u, v, xb3, hist, x3, mod, ws, bs_t, w_pool, pool_scale, w_out)


def _ffn_kernel(x_ref, g_ref, mod_ref, wg_ref, wu_ref, wd_ref, gf_ref, o_ref, h_ref, acc_ref,
                *, final_norm):
    j = pl.program_id(1)

    @pl.when(j == 0)
    def _():
        h_ref[...] = _norm_modulate(x_ref, g_ref, mod_ref, 3, 4).astype(BF16)
        acc_ref[...] = jnp.zeros_like(acc_ref)

    h = h_ref[...]
    a = (_silu(_dot(h, wg_ref[...])) * _dot(h, wu_ref[...])).astype(BF16)
    acc_ref[...] += _dot(a, wd_ref[...])

    @pl.when(j == pl.num_programs(1) - 1)
    def _():
        x = x_ref[...]
        y = x + mod_ref[:, 5:6, :] * acc_ref[...].reshape(x.shape)
        if final_norm:
            ms = jnp.mean(y * y, axis=-1, keepdims=True)
            y = y * lax.rsqrt(ms + EPS) * gf_ref[...][None]
        o_ref[...] = y


def _ffn(x3, g, mod, w_gu, w_down, layer, g_final, groups_per_tile, batch_of_tile, final_norm):
    ng, rn, d = x3.shape
    gt = groups_per_tile
    tm = gt * rn
    dff = w_down.shape[1]
    tf = FFN_TILE
    nf = dff // tf
    return pl.pallas_call(
        functools.partial(_ffn_kernel, final_norm=final_norm),
        grid=(ng // gt, nf),
        in_specs=[
            pl.BlockSpec((gt, rn, d), lambda i, j: (i, 0, 0)),
            _resident((1, d)),
            pl.BlockSpec((gt, 6, d), lambda i, j: (batch_of_tile(i), 0, 0)),
            pl.BlockSpec((None, d, tf), lambda i, j: (layer, 0, j)),
            pl.BlockSpec((None, d, tf), lambda i, j: (layer, 0, j + nf)),
            pl.BlockSpec((None, tf, d), lambda i, j: (layer, j, 0)),
            _resident((1, d)),
        ],
        out_specs=pl.BlockSpec((gt, rn, d), lambda i, j: (i, 0, 0)),
        out_shape=jax.ShapeDtypeStruct(x3.shape, F32),
        scratch_shapes=[pltpu.VMEM((tm, d), BF16), pltpu.VMEM((tm, d), F32)],
        compiler_params=_params("arbitrary", "arbitrary"),
        name="ffn",
    )(x3, g, mod, w_gu, w_gu, w_down, g_final)


def _odd_in_kernel(x_ref, g_ref, mod_ref, w_ref, o_ref, h_ref, cos_ref, sin_ref,
                   *, start, tiles_per_seq, qk_cols, v_cols, head_dim):
    i = pl.program_id(0)
    j = pl.program_id(1)
    tm = h_ref.shape[0]
    tn = o_ref.shape[1]
    half = head_dim // 2
    nq = qk_cols // tn
    nv = v_cols // tn

    def prologue():
        h_ref[...] = _norm_modulate(x_ref, g_ref, mod_ref, 0, 1).astype(BF16)
        gn, rn, _ = x_ref.shape
        row = lax.broadcasted_iota(jnp.int32, (gn, rn, half), 1).reshape(tm, half)
        pos = (row + (start + (i % tiles_per_seq) * rn)).astype(F32)
        lane = lax.broadcasted_iota(jnp.int32, (tm, half), 1).astype(F32)
        ang = pos * jnp.exp(lane * (-math.log(ROPE_BASE) / half))
        cos_ref[...] = jnp.cos(ang)
        sin_ref[...] = jnp.sin(ang)

    def project(epilogue):
        h = h_ref[...]
        for n in range(tn // PROJ_SUB):
            cols = slice(n * PROJ_SUB, (n + 1) * PROJ_SUB)
            epilogue(_dot(h, w_ref[:, cols]), n * PROJ_SUB)

    def rope(scale):
        def epilogue(acc, col0):
            cos = cos_ref[...]
            sin = sin_ref[...]
            for hd in range(PROJ_SUB // head_dim):
                lo = hd * head_dim
                x1 = acc[:, lo:lo + half]
                x2 = acc[:, lo + half:lo + head_dim]
                o_ref[:, col0 + lo:col0 + lo + half] = ((x1 * cos - x2 * sin) * scale).astype(BF16)
                o_ref[:, col0 + lo + half:col0 + lo + head_dim] = (
                    (x1 * sin + x2 * cos) * scale).astype(BF16)
        return epilogue

    def store(fn):
        def epilogue(acc, col0):
            o_ref[:, col0:col0 + PROJ_SUB] = fn(acc).astype(BF16)
        return epilogue

    @pl.when(j == 0)
    def _():
        prologue()
        project(rope(1.0))

    if nq > 1:
        @pl.when(jnp.logical_and(j > 0, j < nq))
        def _():
            project(rope(1.0))

    @pl.when(jnp.logical_and(j >= nq, j < 2 * nq))
    def _():
        project(rope(head_dim ** -0.5))

    @pl.when(jnp.logical_and(j >= 2 * nq, j < 2 * nq + nv))
    def _():
        project(store(lambda a: a))

    @pl.when(j >= 2 * nq + nv)
    def _():
        project(store(_silu))


def _odd_in(x3, g, mod, w_in, groups_per_tile, batch_of_tile, start, tiles_per_seq):
    ng, rn, d = x3.shape
    gt = groups_per_tile
    tm = gt * rn
    m = ng * rn
    n = w_in.shape[1]
    tn = PROJ_TILE
    head_dim = d // RET_HEADS
    kern = functools.partial(_odd_in_kernel, start=start, tiles_per_seq=tiles_per_seq, qk_cols=d,
                             v_cols=2 * d, head_dim=head_dim)
    return pl.pallas_call(
        kern,
        grid=(ng // gt, n // tn),
        in_specs=[
            pl.BlockSpec((gt, rn, d), lambda i, j: (i, 0, 0)),
            _resident((1, d)),
            pl.BlockSpec((gt, 6, d), lambda i, j: (batch_of_tile(i), 0, 0)),
            pl.BlockSpec((d, tn), lambda i, j: (0, j)),
        ],
        out_specs=pl.BlockSpec((tm, tn), lambda i, j: (i, j)),
        out_shape=jax.ShapeDtypeStruct((m, n), BF16),
        scratch_shapes=[pltpu.VMEM((tm, d), BF16), pltpu.VMEM((tm, head_dim // 2), F32),
                        pltpu.VMEM((tm, head_dim // 2), F32)],
        compiler_params=_params("arbitrary", "arbitrary"),
        name="odd_in",
    )(x3, g, mod, w_in)


def _retention_kernel(lg_ref, q_ref, k_ref, v_ref, g_ref, s0_ref, o_ref, sf_ref,
                      s_ref, dec_ref, dq_ref, dk_ref, *, zero_state):
    hg = pl.program_id(1)
    c = pl.program_id(2)
    hb, dk_dim, dv_dim = s_ref.shape
    cl = q_ref.shape[0]
    lanes = dq_ref.shape[-1]

    @pl.when(c == 0)
    def _():
        if zero_state:
            s_ref[...] = jnp.zeros_like(s_ref)
        else:
            s_ref[...] = s0_ref[0]
        li = lax.broadcasted_iota(jnp.int32, (cl, cl), 0)
        mi = lax.broadcasted_iota(jnp.int32, (cl, cl), 1)
        diff = li - mi
        row = lax.broadcasted_iota(jnp.int32, (cl, lanes), 0).astype(F32)
        for hl in range(hb):
            log_g = lg_ref[hg * hb + hl]
            dec = jnp.exp(jnp.maximum(diff, 0).astype(F32) * log_g)
            dec_ref[hl] = jnp.where(diff >= 0, dec, 0.0)
            dq_ref[hl] = jnp.exp((row + 1.0) * log_g)
            dk_ref[hl] = jnp.exp((cl - 1.0 - row) * log_g)

    for hl in range(hb):
        log_g = lg_ref[hg * hb + hl]
        q = q_ref[:, hl * dk_dim:(hl + 1) * dk_dim]
        k = k_ref[:, hl * dk_dim:(hl + 1) * dk_dim]
        v = v_ref[:, hl * dv_dim:(hl + 1) * dv_dim]
        s_prev = s_ref[hl]
        dq = jnp.concatenate([dq_ref[hl]] * (dv_dim // lanes), axis=1)
        dk = jnp.concatenate([dk_ref[hl]] * (dk_dim // lanes), axis=1)
        scores = lax.dot_general(q, k, (((1,), (1,)), ((), ())), preferred_element_type=F32)
        p = (scores * dec_ref[hl]).astype(BF16)
        o = _dot(p, v) + _dot(q, s_prev.astype(BF16)) * dq
        k_dec = (k.astype(F32) * dk).astype(BF16)
        kv = lax.dot_general(k_dec, v, (((0,), (0,)), ((), ())), preferred_element_type=F32)
        s_ref[hl] = s_prev * jnp.exp(jnp.full((1, 1), cl, F32) * log_g) + kv
        o = o * lax.rsqrt(jnp.mean(o * o, axis=-1, keepdims=True) + EPS)
        gate = g_ref[:, hl * dv_dim:(hl + 1) * dv_dim].astype(F32)
        o_ref[:, hl * dv_dim:(hl + 1) * dv_dim] = (gate * o).astype(BF16)

    @pl.when(c == pl.num_programs(2) - 1)
    def _():
        sf_ref[0] = s_ref[...]


def _retention(proj, s0, log_g, batch, seq_len, d_model, zero_state):
    m = proj.shape[0]
    hds = RET_HEADS
    dk_dim = d_model // hds
    dv_dim = 2 * d_model // hds
    cl = min(seq_len, RET_CHUNK)
    nc = seq_len // cl
    hb = RET_HEAD_BLOCK
    qw, vw = hb * dk_dim, hb * dv_dim
    kq = d_model // qw
    kv0 = 2 * d_model // vw
    kg0 = kv0 + 2 * d_model // vw
    lanes = LANES
    state_spec = pl.BlockSpec((1, hb, dk_dim, dv_dim), lambda b, h, c: (b, h, 0, 0))
    s0_spec = _resident(s0.shape) if zero_state else state_spec
    return pl.pallas_call(
        functools.partial(_retention_kernel, zero_state=zero_state),
        grid=(batch, hds // hb, nc),
        in_specs=[
            pl.BlockSpec(memory_space=pltpu.SMEM),
            pl.BlockSpec((cl, qw), lambda b, h, c: (b * nc + c, h)),
            pl.BlockSpec((cl, qw), lambda b, h, c: (b * nc + c, kq + h)),
            pl.BlockSpec((cl, vw), lambda b, h, c: (b * nc + c, kv0 + h)),
            pl.BlockSpec((cl, vw), lambda b, h, c: (b * nc + c, kg0 + h)),
            s0_spec,
        ],
        out_specs=[
            pl.BlockSpec((cl, vw), lambda b, h, c: (b * nc + c, h)),
            state_spec,
        ],
        out_shape=[
            jax.ShapeDtypeStruct((m, 2 * d_model), BF16),
            jax.ShapeDtypeStruct((batch, hds, dk_dim, dv_dim), F32),
        ],
        scratch_shapes=[
            pltpu.VMEM((hb, dk_dim, dv_dim), F32),
            pltpu.VMEM((hb, cl, cl), F32),
            pltpu.VMEM((hb, cl, lanes), F32),
            pltpu.VMEM((hb, cl, lanes), F32),
        ],
        compiler_params=_params("arbitrary", "arbitrary", "arbitrary"),
        name="retention",
    )(log_g, proj, proj, proj, proj, s0)


def _out_proj_kernel(a_ref, x_ref, mod_ref, w_ref, o_ref):
    x = x_ref[...]
    y = _dot(a_ref[...], w_ref[...])
    o_ref[...] = x + mod_ref[:, 2:3, :] * y.reshape(x.shape)


def _out_proj(a, x3, mod, w, groups_per_tile, batch_of_tile):
    ng, rn, d = x3.shape
    gt = groups_per_tile
    tm = gt * rn
    kdim = a.shape[1]
    return pl.pallas_call(
        _out_proj_kernel,
        grid=(ng // gt,),
        in_specs=[
            pl.BlockSpec((tm, kdim), lambda i: (i, 0)),
            pl.BlockSpec((gt, rn, d), lambda i: (i, 0, 0)),
            pl.BlockSpec((gt, 6, d), lambda i: (batch_of_tile(i), 0, 0)),
            _resident(w.shape),
        ],
        out_specs=pl.BlockSpec((gt, rn, d), lambda i: (i, 0, 0)),
        out_shape=jax.ShapeDtypeStruct(x3.shape, F32),
        compiler_params=_params("arbitrary"),
        name="out_proj",
    )(a, x3, mod, w)


def _stream(x, mods, pool_hist, ret_state, start, prm, uv_dtype):
    batch, seq_len, d = x.shape
    m = batch * seq_len
    if seq_len >= ROW_TILE:
        rn, gt = ROW_TILE, 1
        tiles_per_seq = seq_len // ROW_TILE
        batch_of_tile = lambda i: i // tiles_per_seq
    else:
        rn, gt = seq_len, ROW_TILE // seq_len
        tiles_per_seq = 1
        batch_of_tile = lambda i: i
    tm = rn * gt
    x3 = x.reshape(m // rn, rn, d)
    seg = min(seq_len, tm)

    mod = mods[0]
    u, v, xb = _even_in(x3, prm["norm_mix"][0], mod, prm["w_in_ab"], prm["ln_v_g"], prm["ln_v_b"],
                        gt, batch_of_tile, uv_dtype)
    bw = xb.shape[1]
    xb3 = xb.reshape(m // seg, seg, bw)
    if pool_hist is None:
        hist = xb.reshape(m // POOL_HIST, POOL_HIST, bw)
        per_tile = tm // POOL_HIST
        hist_of_tile = lambda i: jnp.maximum(i * per_tile - 1, 0)
        zero_first = True
    else:
        hist = jnp.pad(pool_hist, ((0, 0), (POOL_HIST - POOL_PAD, 0), (0, 0)))
        hist_of_tile = lambda i: i
        zero_first = False
    x3 = _even_mix(u, v, xb3, hist, x3, mod, prm["ws"][seq_len], prm["bs_t"][seq_len],
                   prm["w_pool"], prm["pool_scale"], prm["w_out_ab"], gt, batch_of_tile,
                   hist_of_tile, start, tiles_per_seq, zero_first)
    x3 = _ffn(x3, prm["norm_ffn"][0], mod, prm["w_ffn_gu"], prm["w_ffn_down"], 0,
              prm["norm_final"], gt, batch_of_tile, False)
    pool_state = xb.reshape(batch, seq_len, bw)[:, seq_len - POOL_PAD:]
    v_state = v.reshape(batch, seq_len, -1)

    mod = mods[1]
    proj = _odd_in(x3, prm["norm_mix"][1], mod, prm["w_in_c"], gt, batch_of_tile, start, tiles_per_seq)
    if ret_state is None:
        s0 = jnp.zeros((1, 1, d // RET_HEADS, 2 * d // RET_HEADS), F32)
    else:
        s0 = ret_state
    o, s_fin = _retention(proj, s0, prm["log_g"], batch, seq_len, d, ret_state is None)
    x3 = _out_proj(o, x3, mod, prm["w_out_c"], gt, batch_of_tile)
    y3 = _ffn(x3, prm["norm_ffn"][1], mod, prm["w_ffn_gu"], prm["w_ffn_down"], 1,
              prm["norm_final"], gt, batch_of_tile, True)
    return y3.reshape(batch, seq_len, d), pool_state, v_state, s_fin


def _spatial_weights(w_s, b_s, seq_len):
    blk = min(seq_len, A_BLOCK)
    rep = A_BLOCK // blk
    idx = jnp.arange(A_BLOCK)
    t, s = idx[:, None], idx[None, :]
    same_seq = (t // blk) == (s // blk)
    causal = ((s % blk) // CHUNK) <= ((t % blk) // CHUNK)
    corner = jnp.tile(w_s[:, :blk, :blk], (1, rep, rep))
    ws = jnp.where((same_seq & causal)[None], corner, 0.0).astype(BF16)
    bs_t = jnp.tile(b_s[:, :blk], (1, rep)).T
    return ws, bs_t


def kernel(x_prompt, x_sample, c_prompt, c_sample, state_b_pool, state_c_ret, w_ada, b_ada, norm_mix, norm_ffn, norm_final, w_in_ab, ln_v_g, ln_v_b, w_s, b_s, w_pool, pool_scale, w_out_ab, w_in_c, w_out_c, w_ffn_gu, w_ffn_down):
    bp, sp, d = x_prompt.shape
    bs, ss, _ = x_sample.shape
    depth = w_ada.shape[0]
    past_len = 2048

    rows = bp + bs
    pad_rows = -rows % 16
    c_all = jnp.concatenate([c_prompt, c_sample, jnp.zeros((pad_rows, d), F32)], axis=0)
    ada = _ada(c_all, w_ada, b_ada).reshape(depth, rows + pad_rows, 6, d)
    mods_p = [ada[l, :bp] for l in range(depth)]
    mods_s = [ada[l, bp:rows] for l in range(depth)]

    log_g = jnp.log1p(-jnp.exp2(-5.0 - jnp.arange(RET_HEADS, dtype=F32)))
    prm = {
        "norm_mix": norm_mix.reshape(depth, 1, d),
        "norm_ffn": norm_ffn.reshape(depth, 1, d),
        "norm_final": norm_final.reshape(1, d),
        "w_in_ab": w_in_ab[0].astype(BF16),
        "ln_v_g": ln_v_g[0].reshape(1, -1),
        "ln_v_b": ln_v_b[0].reshape(1, -1),
        "ws": {}, "bs_t": {},
        "w_pool": w_pool[0].astype(BF16),
        "pool_scale": pool_scale[0].reshape(1, -1),
        "w_out_ab": w_out_ab[0].astype(BF16),
        "w_in_c": w_in_c[0].astype(BF16),
        "w_out_c": w_out_c[0].astype(BF16),
        "w_ffn_gu": w_ffn_gu.astype(BF16),
        "w_ffn_down": w_ffn_down.astype(BF16),
        "log_g": log_g,
    }
    for sl in (sp, ss):
        prm["ws"][sl], prm["bs_t"][sl] = _spatial_weights(w_s[0], b_s[0], sl)

    y_p, pool_p, _, ret_p = _stream(x_prompt, mods_p, None, None, 0, prm, BF16)
    y_s, pool_s, v_s, ret_s = _stream(x_sample, mods_s, state_b_pool[0], state_c_ret[0], past_len,
                                      prm, F32)
    return (y_p, y_s, pool_p[None], pool_s[None], v_s[None], ret_p[None], ret_s[None])
```

```python
import functools
import math

import jax
import jax.numpy as jnp
from jax import lax
from jax.experimental import pallas as pl
from jax.experimental.pallas import tpu as pltpu

EPS = 1e-6
CHUNK = 64
A_BLOCK = 128
A_GROUPS = 8
POOL_WINDOWS = (2, 4, 8, 16)
POOL_PAD = 15
POOL_HIST = 16
RET_HEADS = 8
ROPE_BASE = 10000.0
LANES = 128

F32 = jnp.float32
BF16 = jnp.bfloat16

VMEM_LIMIT_BYTES = 56 * 1024 * 1024
ROW_TILE = 512
FFN_ROW_TILE = 1024
FFN_TILE = 512
PROJ_TILE = 2048
PROJ_SUB = 512
FFN_CAST_TILE = 256
PROJ_CAST_TILE = 1024
RET_HEAD_BLOCK = 8
RET_CHUNK = 256


def _params(*sem):
    return pltpu.CompilerParams(dimension_semantics=sem, vmem_limit_bytes=VMEM_LIMIT_BYTES)


def _resident(shape):
    nd = len(shape)
    return pl.BlockSpec(shape, lambda *_: (0,) * nd, pipeline_mode=pl.Buffered(1))


def _dot(a, b):
    return jnp.dot(a, b, preferred_element_type=F32)


def _silu(x):
    return x * jax.nn.sigmoid(x)


def _norm_modulate(x_ref, g_ref, mod_ref, i_shift, i_scale):
    x = x_ref[...]
    gn, rn, dn = x.shape
    ms = jnp.mean(x * x, axis=-1, keepdims=True)
    y = x * lax.rsqrt(ms + EPS) * g_ref[...][None]
    shift = mod_ref[:, i_shift:i_shift + 1, :]
    scale = mod_ref[:, i_scale:i_scale + 1, :]
    h = y * (1.0 + scale) + shift
    return h.reshape(gn * rn, dn)


def _ada_kernel(c_ref, w_ref, b_ref, o_ref):
    s = _silu(c_ref[...]).astype(BF16)
    o_ref[0] = _dot(s, w_ref[0].astype(BF16)) + b_ref[0]


def _ada(c_all, w_ada, b_ada):
    depth, d, n = w_ada.shape
    rows = c_all.shape[0]
    tn = 1024
    return pl.pallas_call(
        _ada_kernel,
        grid=(depth, n // tn),
        in_specs=[
            pl.BlockSpec((rows, d), lambda l, j: (0, 0)),
            pl.BlockSpec((1, d, tn), lambda l, j: (l, 0, j)),
            pl.BlockSpec((1, 1, tn), lambda l, j: (l, 0, j)),
        ],
        out_specs=pl.BlockSpec((1, rows, tn), lambda l, j: (l, 0, j)),
        out_shape=jax.ShapeDtypeStruct((depth, rows, n), F32),
        compiler_params=_params("arbitrary", "arbitrary"),
        name="ada",
    )(c_all, w_ada, b_ada.reshape(depth, 1, n))


def _even_in_kernel(x_ref, g_ref, mod_ref, w_ref, lng_ref, lnb_ref, u_ref, v_ref, xb_ref):
    aw = u_ref.shape[-1]
    h = _norm_modulate(x_ref, g_ref, mod_ref, 0, 1).astype(BF16)
    u = _dot(h, w_ref[:, 0:aw])
    u_ref[...] = jax.nn.gelu(u).astype(u_ref.dtype)
    v = jax.nn.gelu(_dot(h, w_ref[:, aw:2 * aw]))
    mu = jnp.mean(v, axis=-1, keepdims=True)
    vc = v - mu
    vn = vc * lax.rsqrt(jnp.mean(vc * vc, axis=-1, keepdims=True) + EPS)
    v_ref[...] = (vn * lng_ref[...] + lnb_ref[...]).astype(v_ref.dtype)
    xb_ref[...] = _dot(h, w_ref[:, 2 * aw:3 * aw])


def _even_in(x3, g, mod, w_in, ln_g, ln_b, groups_per_tile, batch_of_tile, uv_dtype):
    ng, rn, d = x3.shape
    gt = groups_per_tile
    tm = gt * rn
    m = ng * rn
    aw = w_in.shape[1] // 3
    row = lambda i: (i, 0)
    return pl.pallas_call(
        _even_in_kernel,
        grid=(ng // gt,),
        in_specs=[
            pl.BlockSpec((gt, rn, d), lambda i: (i, 0, 0)),
            _resident((1, d)),
            pl.BlockSpec((gt, 6, d), lambda i: (batch_of_tile(i), 0, 0)),
            _resident(w_in.shape),
            _resident((1, aw)),
            _resident((1, aw)),
        ],
        out_specs=[pl.BlockSpec((tm, aw), row)] * 3,
        out_shape=[
            jax.ShapeDtypeStruct((m, aw), uv_dtype),
            jax.ShapeDtypeStruct((m, aw), uv_dtype),
            jax.ShapeDtypeStruct((m, aw), F32),
        ],
        compiler_params=_params("arbitrary"),
        name="even_in",
    )(x3, g, mod, w_in, ln_g, ln_b)


def _even_mix_kernel(u_ref, v_ref, xb_ref, hist_ref, x_ref, mod_ref, ws_ref, bs_ref, wp_ref,
                     ps_ref, wo_ref, o_ref, y_ref, *, start, tiles_per_seq, zero_first_hist):
    i = pl.program_id(0)
    tm, aw = u_ref.shape
    nseg, seg, bw = xb_ref.shape
    nblk = tm // A_BLOCK
    gdim = aw // A_GROUPS

    for g in range(A_GROUPS):
        cols = slice(g * gdim, (g + 1) * gdim)
        vg = jnp.concatenate(
            [v_ref[n * A_BLOCK:(n + 1) * A_BLOCK, cols].astype(BF16) for n in range(nblk)], axis=1)
        mixed = _dot(ws_ref[g], vg) + bs_ref[:, g:g + 1]
        for n in range(nblk):
            rows = slice(n * A_BLOCK, (n + 1) * A_BLOCK)
            piece = mixed[:, n * gdim:(n + 1) * gdim]
            y_ref[rows, cols] = (u_ref[rows, cols].astype(F32) * piece).astype(BF16)

    hist = hist_ref[...]
    if zero_first_hist:
        hist = jnp.where(i % tiles_per_seq == 0, 0.0, hist)
    ext = jnp.concatenate([hist, xb_ref[...]], axis=1)
    off = start + (i % tiles_per_seq) * seg if tiles_per_seq > 1 else start
    posp1 = lax.broadcasted_iota(jnp.int32, (1, seg, 1), 1) + (off + 1)
    pdim = bw // len(POOL_WINDOWS)
    for g, w in enumerate(POOL_WINDOWS):
        cols = slice(g * pdim, (g + 1) * pdim)
        a = ext[:, :, cols]
        tot = a
        k = 1
        while k < w:
            tot = tot + pltpu.roll(tot, k, axis=1)
            k *= 2
        inv = 1.0 / jnp.minimum(posp1, w).astype(F32)
        pooled = tot[:, POOL_HIST:, :] * inv - a[:, POOL_HIST:, :]
        pooled = pooled.reshape(tm, pdim).astype(BF16)
        yb = _dot(pooled, wp_ref[g]) * ps_ref[:, cols]
        y_ref[:, aw + g * pdim:aw + (g + 1) * pdim] = yb.astype(BF16)

    out = _dot(y_ref[...], wo_ref[...])
    x = x_ref[...]
    gate = mod_ref[:, 2:3, :]
    o_ref[...] = x + gate * out.reshape(x.shape)


def _even_mix(u, v, xb3, hist, x3, mod, ws, bs_t, w_pool, pool_scale, w_out, groups_per_tile,
              batch_of_tile, hist_of_tile, start, tiles_per_seq, zero_first_hist):
    ng, rn, d = x3.shape
    gt = groups_per_tile
    tm = gt * rn
    m, aw = u.shape
    nseg_total, seg, bw = xb3.shape
    nseg = tm // seg
    kern = functools.partial(_even_mix_kernel, start=start, tiles_per_seq=tiles_per_seq,
                             zero_first_hist=zero_first_hist)
    return pl.pallas_call(
        kern,
        grid=(m // tm,),
        in_specs=[
            pl.BlockSpec((tm, aw), lambda i: (i, 0)),
            pl.BlockSpec((tm, aw), lambda i: (i, 0)),
            pl.BlockSpec((nseg, seg, bw), lambda i: (i, 0, 0)),
            pl.BlockSpec((nseg, POOL_HIST, bw), lambda i: (hist_of_tile(i), 0, 0)),
            pl.BlockSpec((gt, rn, d), lambda i: (i, 0, 0)),
            pl.BlockSpec((gt, 6, d), lambda i: (batch_of_tile(i), 0, 0)),
            _resident(ws.shape),
            _resident(bs_t.shape),
            _resident(w_pool.shape),
            _resident(pool_scale.shape),
            _resident(w_out.shape),
        ],
        out_specs=pl.BlockSpec((gt, rn, d), lambda i: (i, 0, 0)),
        out_shape=jax.ShapeDtypeStruct(x3.shape, F32),
        scratch_shapes=[pltpu.VMEM((tm, aw + bw), BF16)],
        compiler_params=_params("arbitrary"),
        name="even_mix",
    )(u, v, xb3, hist, x3, mod, ws, bs_t, w_pool, pool_scale, w_out)


def _ffn_kernel(x_ref, g_ref, mod_ref, wg_ref, wu_ref, wd_ref, gf_ref, o_ref, *rest,
                final_norm, emit_bf16):
    j = pl.program_id(1)
    last = pl.num_programs(1) - 1
    h_ref = rest[-1]

    def weights():
        if not emit_bf16:
            return wg_ref[...], wu_ref[...], wd_ref[...]
        tiles = tuple(r[...].astype(BF16) for r in (wg_ref, wu_ref, wd_ref))
        for out_ref, tile in zip(rest[:3], tiles):
            out_ref[...] = tile
        return tiles

    def step(first, final):
        if first:
            h = _norm_modulate(x_ref, g_ref, mod_ref, 3, 4).astype(BF16)
            h_ref[...] = h
        else:
            h = h_ref[...]
        wg, wu, wd = weights()
        a = (_silu(_dot(h, wg)) * _dot(h, wu)).astype(BF16)
        y = _dot(a, wd).reshape(o_ref.shape)
        acc = y if first else o_ref[...] + y
        if final:
            acc = x_ref[...] + mod_ref[:, 5:6, :] * acc
            if final_norm:
                ms = jnp.mean(acc * acc, axis=-1, keepdims=True)
                acc = acc * lax.rsqrt(ms + EPS) * gf_ref[...][None]
        o_ref[...] = acc

    pl.when(j == 0)(lambda: step(True, False))
    pl.when(jnp.logical_and(j > 0, j < last))(lambda: step(False, False))
    pl.when(j == last)(lambda: step(False, True))


def _ffn(x3, g, mod, weights, layer, g_final, groups_per_tile, batch_of_tile, final_norm):
    ng, rn, d = x3.shape
    gt = groups_per_tile
    tm = gt * rn
    emit = len(weights) == 2
    x_spec = pl.BlockSpec((gt, rn, d), lambda i, j: (i, 0, 0))
    if emit:
        w_gu, w_down = weights
        dff = w_down.shape[1]
        tf = FFN_CAST_TILE
        nf = dff // tf
        w_specs = [
            pl.BlockSpec((None, d, tf), lambda i, j: (layer, 0, j)),
            pl.BlockSpec((None, d, tf), lambda i, j: (layer, 0, j + nf)),
            pl.BlockSpec((None, tf, d), lambda i, j: (layer, j, 0)),
        ]
        w_args = (w_gu, w_gu, w_down)
    else:
        dff = weights[2].shape[0]
        tf = FFN_TILE
        nf = dff // tf
        w_args = weights
    bf_specs = [
        pl.BlockSpec((d, tf), lambda i, j: (0, j)),
        pl.BlockSpec((d, tf), lambda i, j: (0, j)),
        pl.BlockSpec((tf, d), lambda i, j: (j, 0)),
    ]
    if not emit:
        w_specs = bf_specs
    out_specs = [x_spec]
    out_shape = [jax.ShapeDtypeStruct(x3.shape, F32)]
    if emit:
        out_specs += bf_specs
        out_shape += [jax.ShapeDtypeStruct((d, dff), BF16), jax.ShapeDtypeStruct((d, dff), BF16),
                      jax.ShapeDtypeStruct((dff, d), BF16)]
    outs = pl.pallas_call(
        functools.partial(_ffn_kernel, final_norm=final_norm, emit_bf16=emit),
        grid=(ng // gt, nf),
        in_specs=[
            x_spec,
            _resident((1, d)),
            pl.BlockSpec((gt, 6, d), lambda i, j: (batch_of_tile(i), 0, 0)),
            *w_specs,
            _resident((1, d)),
        ],
        out_specs=out_specs,
        out_shape=out_shape,
        scratch_shapes=[pltpu.VMEM((tm, d), BF16)],
        compiler_params=_params("arbitrary", "arbitrary"),
        name="ffn",
    )(x3, g, mod, *w_args, g_final)
    return outs[0], tuple(outs[1:])


def _odd_in_kernel(x_ref, g_ref, mod_ref, w_ref, o_ref, *rest,
                   start, tiles_per_seq, qk_cols, v_cols, head_dim, emit_bf16):
    i = pl.program_id(0)
    j = pl.program_id(1)
    h_ref, cos_ref, sin_ref = rest[-3:]
    tm = h_ref.shape[0]
    tn = o_ref.shape[1]
    half = head_dim // 2
    nq = qk_cols // tn
    nv = v_cols // tn

    def prologue():
        h_ref[...] = _norm_modulate(x_ref, g_ref, mod_ref, 0, 1).astype(BF16)
        gn, rn, _ = x_ref.shape
        row = lax.broadcasted_iota(jnp.int32, (gn, rn, half), 1).reshape(tm, half)
        pos = (row + (start + (i % tiles_per_seq) * rn)).astype(F32)
        lane = lax.broadcasted_iota(jnp.int32, (tm, half), 1).astype(F32)
        ang = pos * jnp.exp(lane * (-math.log(ROPE_BASE) / half))
        cos_ref[...] = jnp.cos(ang)
        sin_ref[...] = jnp.sin(ang)

    def project(epilogue):
        h = h_ref[...]
        for n in range(tn // PROJ_SUB):
            cols = slice(n * PROJ_SUB, (n + 1) * PROJ_SUB)
            w = w_ref[:, cols]
            if emit_bf16:
                w = w.astype(BF16)
                rest[0][:, cols] = w
            epilogue(_dot(h, w), n * PROJ_SUB)

    def rope(scale):
        def epilogue(acc, col0):
            cos = cos_ref[...]
            sin = sin_ref[...]
            for hd in range(PROJ_SUB // head_dim):
                lo = hd * head_dim
                x1 = acc[:, lo:lo + half]
                x2 = acc[:, lo + half:lo + head_dim]
                o_ref[:, col0 + lo:col0 + lo + half] = ((x1 * cos - x2 * sin) * scale).astype(BF16)
                o_ref[:, col0 + lo + half:col0 + lo + head_dim] = (
                    (x1 * sin + x2 * cos) * scale).astype(BF16)
        return epilogue

    def store(fn):
        def epilogue(acc, col0):
            o_ref[:, col0:col0 + PROJ_SUB] = fn(acc).astype(BF16)
        return epilogue

    @pl.when(j == 0)
    def _():
        prologue()
        project(rope(1.0))

    if nq > 1:
        @pl.when(jnp.logical_and(j > 0, j < nq))
        def _():
            project(rope(1.0))

    @pl.when(jnp.logical_and(j >= nq, j < 2 * nq))
    def _():
        project(rope(head_dim ** -0.5))

    @pl.when(jnp.logical_and(j >= 2 * nq, j < 2 * nq + nv))
    def _():
        project(store(lambda a: a))

    @pl.when(j >= 2 * nq + nv)
    def _():
        project(store(_silu))


def _odd_in(x3, g, mod, w_in, groups_per_tile, batch_of_tile, start, tiles_per_seq):
    ng, rn, d = x3.shape
    gt = groups_per_tile
    tm = gt * rn
    m = ng * rn
    n = w_in.shape[1]
    emit = w_in.dtype != BF16
    tn = PROJ_CAST_TILE if emit else PROJ_TILE
    head_dim = d // RET_HEADS
    kern = functools.partial(_odd_in_kernel, start=start, tiles_per_seq=tiles_per_seq, qk_cols=d,
                             v_cols=2 * d, head_dim=head_dim, emit_bf16=emit)
    w_spec = pl.BlockSpec((d, tn), lambda i, j: (0, j))
    out_specs = [pl.BlockSpec((tm, tn), lambda i, j: (i, j))]
    out_shape = [jax.ShapeDtypeStruct((m, n), BF16)]
    if emit:
        out_specs.append(w_spec)
        out_shape.append(jax.ShapeDtypeStruct((d, n), BF16))
    outs = pl.pallas_call(
        kern,
        grid=(ng // gt, n // tn),
        in_specs=[
            pl.BlockSpec((gt, rn, d), lambda i, j: (i, 0, 0)),
            _resident((1, d)),
            pl.BlockSpec((gt, 6, d), lambda i, j: (batch_of_tile(i), 0, 0)),
            w_spec,
        ],
        out_specs=out_specs,
        out_shape=out_shape,
        scratch_shapes=[pltpu.VMEM((tm, d), BF16), pltpu.VMEM((tm, head_dim // 2), F32),
                        pltpu.VMEM((tm, head_dim // 2), F32)],
        compiler_params=_params("arbitrary", "arbitrary"),
        name="odd_in",
    )(x3, g, mod, w_in)
    return outs[0], (outs[1] if emit else w_in)


def _retention_kernel(lg_ref, q_ref, k_ref, v_ref, g_ref, s0_ref, o_ref, sf_ref,
                      s_ref, dec_ref, dq_ref, dk_ref, *, zero_state):
    hg = pl.program_id(1)
    c = pl.program_id(2)
    hb, dk_dim, dv_dim = s_ref.shape
    cl = q_ref.shape[0]
    lanes = dq_ref.shape[-1]

    @pl.when(c == 0)
    def _():
        if zero_state:
            s_ref[...] = jnp.zeros_like(s_ref)
        else:
            s_ref[...] = s0_ref[0]
        li = lax.broadcasted_iota(jnp.int32, (cl, cl), 0)
        mi = lax.broadcasted_iota(jnp.int32, (cl, cl), 1)
        diff = li - mi
        row = lax.broadcasted_iota(jnp.int32, (cl, lanes), 0).astype(F32)
        for hl in range(hb):
            log_g = lg_ref[hg * hb + hl]
            dec = jnp.exp(jnp.maximum(diff, 0).astype(F32) * log_g)
            dec_ref[hl] = jnp.where(diff >= 0, dec, 0.0)
            dq_ref[hl] = jnp.exp((row + 1.0) * log_g)
            dk_ref[hl] = jnp.exp((cl - 1.0 - row) * log_g)

    for hl in range(hb):
        log_g = lg_ref[hg * hb + hl]
        q = q_ref[:, hl * dk_dim:(hl + 1) * dk_dim]
        k = k_ref[:, hl * dk_dim:(hl + 1) * dk_dim]
        v = v_ref[:, hl * dv_dim:(hl + 1) * dv_dim]
        s_prev = s_ref[hl]
        dq = jnp.concatenate([dq_ref[hl]] * (dv_dim // lanes), axis=1)
        dk = jnp.concatenate([dk_ref[hl]] * (dk_dim // lanes), axis=1)
        scores = lax.dot_general(q, k, (((1,), (1,)), ((), ())), preferred_element_type=F32)
        p = (scores * dec_ref[hl]).astype(BF16)
        o = _dot(p, v) + _dot(q, s_prev.astype(BF16)) * dq
        k_dec = (k.astype(F32) * dk).astype(BF16)
        kv = lax.dot_general(k_dec, v, (((0,), (0,)), ((), ())), preferred_element_type=F32)
        s_ref[hl] = s_prev * jnp.exp(jnp.full((1, 1), cl, F32) * log_g) + kv
        o = o * lax.rsqrt(jnp.mean(o * o, axis=-1, keepdims=True) + EPS)
        gate = g_ref[:, hl * dv_dim:(hl + 1) * dv_dim].astype(F32)
        o_ref[:, hl * dv_dim:(hl + 1) * dv_dim] = (gate * o).astype(BF16)

    @pl.when(c == pl.num_programs(2) - 1)
    def _():
        sf_ref[0] = s_ref[...]


def _retention(proj, s0, log_g, batch, seq_len, d_model, zero_state):
    m = proj.shape[0]
    hds = RET_HEADS
    dk_dim = d_model // hds
    dv_dim = 2 * d_model // hds
    cl = min(seq_len, RET_CHUNK)
    nc = seq_len // cl
    hb = RET_HEAD_BLOCK
    qw, vw = hb * dk_dim, hb * dv_dim
    kq = d_model // qw
    kv0 = 2 * d_model // vw
    kg0 = kv0 + 2 * d_model // vw
    lanes = LANES
    state_spec = pl.BlockSpec((1, hb, dk_dim, dv_dim), lambda b, h, c: (b, h, 0, 0))
    s0_spec = _resident(s0.shape) if zero_state else state_spec
    return pl.pallas_call(
        functools.partial(_retention_kernel, zero_state=zero_state),
        grid=(batch, hds // hb, nc),
        in_specs=[
            pl.BlockSpec(memory_space=pltpu.SMEM),
            pl.BlockSpec((cl, qw), lambda b, h, c: (b * nc + c, h)),
            pl.BlockSpec((cl, qw), lambda b, h, c: (b * nc + c, kq + h)),
            pl.BlockSpec((cl, vw), lambda b, h, c: (b * nc + c, kv0 + h)),
            pl.BlockSpec((cl, vw), lambda b, h, c: (b * nc + c, kg0 + h)),
            s0_spec,
        ],
        out_specs=[
            pl.BlockSpec((cl, vw), lambda b, h, c: (b * nc + c, h)),
            state_spec,
        ],
        out_shape=[
            jax.ShapeDtypeStruct((m, 2 * d_model), BF16),
            jax.ShapeDtypeStruct((batch, hds, dk_dim, dv_dim), F32),
        ],
        scratch_shapes=[
            pltpu.VMEM((hb, dk_dim, dv_dim), F32),
            pltpu.VMEM((hb, cl, cl), F32),
            pltpu.VMEM((hb, cl, lanes), F32),
            pltpu.VMEM((hb, cl, lanes), F32),
        ],
        compiler_params=_params("arbitrary", "arbitrary", "arbitrary"),
        name="retention",
    )(log_g, proj, proj, proj, proj, s0)


def _out_proj_kernel(a_ref, x_ref, mod_ref, w_ref, o_ref):
    x = x_ref[...]
    y = _dot(a_ref[...], w_ref[...])
    o_ref[...] = x + mod_ref[:, 2:3, :] * y.reshape(x.shape)


def _out_proj(a, x3, mod, w, groups_per_tile, batch_of_tile):
    ng, rn, d = x3.shape
    gt = groups_per_tile
    tm = gt * rn
    kdim = a.shape[1]
    return pl.pallas_call(
        _out_proj_kernel,
        grid=(ng // gt,),
        in_specs=[
            pl.BlockSpec((tm, kdim), lambda i: (i, 0)),
            pl.BlockSpec((gt, rn, d), lambda i: (i, 0, 0)),
            pl.BlockSpec((gt, 6, d), lambda i: (batch_of_tile(i), 0, 0)),
            _resident(w.shape),
        ],
        out_specs=pl.BlockSpec((gt, rn, d), lambda i: (i, 0, 0)),
        out_shape=jax.ShapeDtypeStruct(x3.shape, F32),
        compiler_params=_params("arbitrary"),
        name="out_proj",
    )(a, x3, mod, w)


def _tiling(batch, seq_len, row_tile):
    tile = min(row_tile, batch * seq_len)
    if seq_len >= tile:
        tiles_per_seq = seq_len // tile
        return tile, 1, tiles_per_seq, lambda i: i // tiles_per_seq
    return seq_len, tile // seq_len, 1, lambda i: i


def _stream(x, mods, pool_hist, ret_state, start, prm, heavy, uv_dtype):
    heavy_bf = {"ffn": [None] * len(heavy["ffn"])}
    batch, seq_len, d = x.shape
    m = batch * seq_len
    rn, gt, tiles_per_seq, batch_of_tile = _tiling(batch, seq_len, ROW_TILE)
    tm = rn * gt
    x3 = x.reshape(m // rn, rn, d)
    seg = min(seq_len, tm)
    rn_f, gt_f, _, batch_of_ffn_tile = _tiling(batch, seq_len, FFN_ROW_TILE)

    def ffn(x3, layer, final_norm):
        y, emitted = _ffn(x3.reshape(m // rn_f, rn_f, d), prm["norm_ffn"][layer], mods[layer],
                          heavy["ffn"][layer], layer, prm["norm_final"], gt_f, batch_of_ffn_tile,
                          final_norm)
        heavy_bf["ffn"][layer] = emitted or heavy["ffn"][layer]
        return y.reshape(x3.shape)

    mod = mods[0]
    u, v, xb = _even_in(x3, prm["norm_mix"][0], mod, prm["w_in_ab"], prm["ln_v_g"], prm["ln_v_b"],
                        gt, batch_of_tile, uv_dtype)
    bw = xb.shape[1]
    xb3 = xb.reshape(m // seg, seg, bw)
    if pool_hist is None:
        hist = xb.reshape(m // POOL_HIST, POOL_HIST, bw)
        per_tile = tm // POOL_HIST
        hist_of_tile = lambda i: jnp.maximum(i * per_tile - 1, 0)
        zero_first = True
    else:
        hist = jnp.pad(pool_hist, ((0, 0), (POOL_HIST - POOL_PAD, 0), (0, 0)))
        hist_of_tile = lambda i: i
        zero_first = False
    x3 = _even_mix(u, v, xb3, hist, x3, mod, prm["ws"][seq_len], prm["bs_t"][seq_len],
                   prm["w_pool"], prm["pool_scale"], prm["w_out_ab"], gt, batch_of_tile,
                   hist_of_tile, start, tiles_per_seq, zero_first)
    x3 = ffn(x3, 0, False)
    pool_state = xb.reshape(batch, seq_len, bw)[:, seq_len - POOL_PAD:]
    v_state = v.reshape(batch, seq_len, -1)

    mod = mods[1]
    proj, heavy_bf["w_in_c"] = _odd_in(x3, prm["norm_mix"][1], mod, heavy["w_in_c"], gt,
                                       batch_of_tile, start, tiles_per_seq)
    if ret_state is None:
        s0 = jnp.zeros((1, 1, d // RET_HEADS, 2 * d // RET_HEADS), F32)
    else:
        s0 = ret_state
    o, s_fin = _retention(proj, s0, prm["log_g"], batch, seq_len, d, ret_state is None)
    x3 = _out_proj(o, x3, mod, prm["w_out_c"], gt, batch_of_tile)
    y3 = ffn(x3, 1, True)
    return (y3.reshape(batch, seq_len, d), pool_state, v_state, s_fin), heavy_bf


def _spatial_weights(w_s, b_s, seq_len):
    blk = min(seq_len, A_BLOCK)
    rep = A_BLOCK // blk
    idx = jnp.arange(A_BLOCK)
    t, s = idx[:, None], idx[None, :]
    same_seq = (t // blk) == (s // blk)
    causal = ((s % blk) // CHUNK) <= ((t % blk) // CHUNK)
    corner = jnp.tile(w_s[:, :blk, :blk], (1, rep, rep))
    ws = jnp.where((same_seq & causal)[None], corner, 0.0).astype(BF16)
    bs_t = jnp.tile(b_s[:, :blk], (1, rep)).T
    return ws, bs_t


def kernel(x_prompt, x_sample, c_prompt, c_sample, state_b_pool, state_c_ret, w_ada, b_ada, norm_mix, norm_ffn, norm_final, w_in_ab, ln_v_g, ln_v_b, w_s, b_s, w_pool, pool_scale, w_out_ab, w_in_c, w_out_c, w_ffn_gu, w_ffn_down):
    bp, sp, d = x_prompt.shape
    bs, ss, _ = x_sample.shape
    depth = w_ada.shape[0]
    past_len = 2048

    rows = bp + bs
    pad_rows = -rows % 16
    c_all = jnp.concatenate([c_prompt, c_sample, jnp.zeros((pad_rows, d), F32)], axis=0)
    ada = _ada(c_all, w_ada, b_ada).reshape(depth, rows + pad_rows, 6, d)
    mods_p = [ada[l, :bp] for l in range(depth)]
    mods_s = [ada[l, bp:rows] for l in range(depth)]

    log_g = jnp.log1p(-jnp.exp2(-5.0 - jnp.arange(RET_HEADS, dtype=F32)))
    prm = {
        "norm_mix": norm_mix.reshape(depth, 1, d),
        "norm_ffn": norm_ffn.reshape(depth, 1, d),
        "norm_final": norm_final.reshape(1, d),
        "w_in_ab": w_in_ab[0].astype(BF16),
        "ln_v_g": ln_v_g[0].reshape(1, -1),
        "ln_v_b": ln_v_b[0].reshape(1, -1),
        "ws": {}, "bs_t": {},
        "w_pool": w_pool[0].astype(BF16),
        "pool_scale": pool_scale[0].reshape(1, -1),
        "w_out_ab": w_out_ab[0].astype(BF16),
        "w_out_c": w_out_c[0].astype(BF16),
        "log_g": log_g,
    }
    for sl in (sp, ss):
        prm["ws"][sl], prm["bs_t"][sl] = _spatial_weights(w_s[0], b_s[0], sl)

    heavy = {"ffn": [(w_ffn_gu, w_ffn_down)] * depth, "w_in_c": w_in_c[0]}
    (y_s, pool_s, v_s, ret_s), heavy_bf = _stream(x_sample, mods_s, state_b_pool[0], state_c_ret[0],
                                                  past_len, prm, heavy, F32)
    (y_p, pool_p, _, ret_p), _ = _stream(x_prompt, mods_p, None, None, 0, prm, heavy_bf, BF16)
    return (y_p, y_s, pool_p[None], pool_s[None], v_s[None], ret_p[None], ret_s[None])
```

```python
import functools
import math

import jax
import jax.numpy as jnp
from jax import lax
from jax.experimental import pallas as pl
from jax.experimental.pallas import tpu as pltpu

EPS = 1e-6
CHUNK = 64
A_BLOCK = 128
A_GROUPS = 8
POOL_WINDOWS = (2, 4, 8, 16)
POOL_PAD = 15
POOL_HIST = 16
RET_HEADS = 8
ROPE_BASE = 10000.0
LANES = 128

F32 = jnp.float32
BF16 = jnp.bfloat16

VMEM_LIMIT_BYTES = 56 * 1024 * 1024
ROW_TILE = 512
FFN_ROW_TILE = 1024
PROJ_ROW_TILE = 1024
FFN_TILE = 512
PROJ_TILE = 2048
PROJ_SUB = 512
FFN_CAST_TILE = 256
PROJ_CAST_TILE = 1024
RET_CHUNK = 256
RET_PROJ_HEADS = 4


def _params(*sem):
    return pltpu.CompilerParams(dimension_semantics=sem, vmem_limit_bytes=VMEM_LIMIT_BYTES)


def _resident(shape):
    nd = len(shape)
    return pl.BlockSpec(shape, lambda *_: (0,) * nd, pipeline_mode=pl.Buffered(1))


def _dot(a, b):
    return jnp.dot(a, b, preferred_element_type=F32)


def _silu(x):
    return x * jax.nn.sigmoid(x)


def _norm_modulate(x_ref, g_ref, mod_ref, i_shift, i_scale):
    x = x_ref[...]
    gn, rn, dn = x.shape
    ms = jnp.mean(x * x, axis=-1, keepdims=True)
    y = x * lax.rsqrt(ms + EPS) * g_ref[...][None]
    shift = mod_ref[:, i_shift:i_shift + 1, :]
    scale = mod_ref[:, i_scale:i_scale + 1, :]
    h = y * (1.0 + scale) + shift
    return h.reshape(gn * rn, dn)


def _ada_kernel(c_ref, w_ref, b_ref, o_ref):
    s = _silu(c_ref[...]).astype(BF16)
    o_ref[0] = _dot(s, w_ref[0].astype(BF16)) + b_ref[0]


def _ada(c_all, w_ada, b_ada):
    depth, d, n = w_ada.shape
    rows = c_all.shape[0]
    tn = 1024
    return pl.pallas_call(
        _ada_kernel,
        grid=(depth, n // tn),
        in_specs=[
            pl.BlockSpec((rows, d), lambda l, j: (0, 0)),
            pl.BlockSpec((1, d, tn), lambda l, j: (l, 0, j)),
            pl.BlockSpec((1, 1, tn), lambda l, j: (l, 0, j)),
        ],
        out_specs=pl.BlockSpec((1, rows, tn), lambda l, j: (l, 0, j)),
        out_shape=jax.ShapeDtypeStruct((depth, rows, n), F32),
        compiler_params=_params("arbitrary", "arbitrary"),
        name="ada",
    )(c_all, w_ada, b_ada.reshape(depth, 1, n))


def _even_in_kernel(x_ref, g_ref, mod_ref, w_ref, lng_ref, lnb_ref, u_ref, v_ref, xb_ref):
    aw = u_ref.shape[-1]
    h = _norm_modulate(x_ref, g_ref, mod_ref, 0, 1).astype(BF16)
    u = _dot(h, w_ref[:, 0:aw])
    u_ref[...] = jax.nn.gelu(u).astype(u_ref.dtype)
    v = jax.nn.gelu(_dot(h, w_ref[:, aw:2 * aw]))
    mu = jnp.mean(v, axis=-1, keepdims=True)
    vc = v - mu
    vn = vc * lax.rsqrt(jnp.mean(vc * vc, axis=-1, keepdims=True) + EPS)
    v_ref[...] = (vn * lng_ref[...] + lnb_ref[...]).astype(v_ref.dtype)
    xb_ref[...] = _dot(h, w_ref[:, 2 * aw:3 * aw])


def _even_in(x3, g, mod, w_in, ln_g, ln_b, groups_per_tile, batch_of_tile, uv_dtype):
    ng, rn, d = x3.shape
    gt = groups_per_tile
    tm = gt * rn
    m = ng * rn
    aw = w_in.shape[1] // 3
    row = lambda i: (i, 0)
    return pl.pallas_call(
        _even_in_kernel,
        grid=(ng // gt,),
        in_specs=[
            pl.BlockSpec((gt, rn, d), lambda i: (i, 0, 0)),
            _resident((1, d)),
            pl.BlockSpec((gt, 6, d), lambda i: (batch_of_tile(i), 0, 0)),
            _resident(w_in.shape),
            _resident((1, aw)),
            _resident((1, aw)),
        ],
        out_specs=[pl.BlockSpec((tm, aw), row)] * 3,
        out_shape=[
            jax.ShapeDtypeStruct((m, aw), uv_dtype),
            jax.ShapeDtypeStruct((m, aw), uv_dtype),
            jax.ShapeDtypeStruct((m, aw), F32),
        ],
        compiler_params=_params("arbitrary"),
        name="even_in",
    )(x3, g, mod, w_in, ln_g, ln_b)


def _even_mix_kernel(u_ref, v_ref, xb_ref, hist_ref, x_ref, mod_ref, ws_ref, bs_ref, wp_ref,
                     ps_ref, wo_ref, o_ref, y_ref, *, start, tiles_per_seq, zero_first_hist):
    i = pl.program_id(0)
    tm, aw = u_ref.shape
    nseg, seg, bw = xb_ref.shape
    nblk = tm // A_BLOCK
    gdim = aw // A_GROUPS

    for g in range(A_GROUPS):
        cols = slice(g * gdim, (g + 1) * gdim)
        vg = jnp.concatenate(
            [v_ref[n * A_BLOCK:(n + 1) * A_BLOCK, cols].astype(BF16) for n in range(nblk)], axis=1)
        mixed = _dot(ws_ref[g], vg) + bs_ref[:, g:g + 1]
        for n in range(nblk):
            rows = slice(n * A_BLOCK, (n + 1) * A_BLOCK)
            piece = mixed[:, n * gdim:(n + 1) * gdim]
            y_ref[rows, cols] = (u_ref[rows, cols].astype(F32) * piece).astype(BF16)

    hist = hist_ref[...]
    if zero_first_hist:
        hist = jnp.where(i % tiles_per_seq == 0, 0.0, hist)
    ext = jnp.concatenate([hist, xb_ref[...]], axis=1)
    off = start + (i % tiles_per_seq) * seg if tiles_per_seq > 1 else start
    posp1 = lax.broadcasted_iota(jnp.int32, (1, seg, 1), 1) + (off + 1)
    pdim = bw // len(POOL_WINDOWS)
    for g, w in enumerate(POOL_WINDOWS):
        cols = slice(g * pdim, (g + 1) * pdim)
        a = ext[:, :, cols]
        tot = a
        k = 1
        while k < w:
            tot = tot + pltpu.roll(tot, k, axis=1)
            k *= 2
        inv = 1.0 / jnp.minimum(posp1, w).astype(F32)
        pooled = tot[:, POOL_HIST:, :] * inv - a[:, POOL_HIST:, :]
        pooled = pooled.reshape(tm, pdim).astype(BF16)
        yb = _dot(pooled, wp_ref[g]) * ps_ref[:, cols]
        y_ref[:, aw + g * pdim:aw + (g + 1) * pdim] = yb.astype(BF16)

    out = _dot(y_ref[...], wo_ref[...])
    x = x_ref[...]
    gate = mod_ref[:, 2:3, :]
    o_ref[...] = x + gate * out.reshape(x.shape)


def _even_mix(u, v, xb3, hist, x3, mod, ws, bs_t, w_pool, pool_scale, w_out, groups_per_tile,
              batch_of_tile, hist_of_tile, start, tiles_per_seq, zero_first_hist):
    ng, rn, d = x3.shape
    gt = groups_per_tile
    tm = gt * rn
    m, aw = u.shape
    nseg_total, seg, bw = xb3.shape
    nseg = tm // seg
    kern = functools.partial(_even_mix_kernel, start=start, tiles_per_seq=tiles_per_seq,
                             zero_first_hist=zero_first_hist)
    return pl.pallas_call(
        kern,
        grid=(m // tm,),
        in_specs=[
            pl.BlockSpec((tm, aw), lambda i: (i, 0)),
            pl.BlockSpec((tm, aw), lambda i: (i, 0)),
            pl.BlockSpec((nseg, seg, bw), lambda i: (i, 0, 0)),
            pl.BlockSpec((nseg, POOL_HIST, bw), lambda i: (hist_of_tile(i), 0, 0)),
            pl.BlockSpec((gt, rn, d), lambda i: (i, 0, 0)),
            pl.BlockSpec((gt, 6, d), lambda i: (batch_of_tile(i), 0, 0)),
            _resident(ws.shape),
            _resident(bs_t.shape),
            _resident(w_pool.shape),
            _resident(pool_scale.shape),
            _resident(w_out.shape),
        ],
        out_specs=pl.BlockSpec((gt, rn, d), lambda i: (i, 0, 0)),
        out_shape=jax.ShapeDtypeStruct(x3.shape, F32),
        scratch_shapes=[pltpu.VMEM((tm, aw + bw), BF16)],
        compiler_params=_params("arbitrary"),
        name="even_mix",
    )(u, v, xb3, hist, x3, mod, ws, bs_t, w_pool, pool_scale, w_out)


def _ffn_kernel(x_ref, g_ref, mod_ref, wg_ref, wu_ref, wd_ref, gf_ref, o_ref, *rest,
                final_norm, emit_bf16):
    j = pl.program_id(1)
    last = pl.num_programs(1) - 1
    h_ref = rest[-1]

    def weights():
        if not emit_bf16:
            return wg_ref[...], wu_ref[...], wd_ref[...]
        tiles = tuple(r[...].astype(BF16) for r in (wg_ref, wu_ref, wd_ref))
        for out_ref, tile in zip(rest[:3], tiles):
            out_ref[...] = tile
        return tiles

    def step(first, final):
        if first:
            h = _norm_modulate(x_ref, g_ref, mod_ref, 3, 4).astype(BF16)
            h_ref[...] = h
        else:
            h = h_ref[...]
        wg, wu, wd = weights()
        a = (_silu(_dot(h, wg)) * _dot(h, wu)).astype(BF16)
        y = _dot(a, wd).reshape(o_ref.shape)
        acc = y if first else o_ref[...] + y
        if final:
            acc = x_ref[...] + mod_ref[:, 5:6, :] * acc
            if final_norm:
                ms = jnp.mean(acc * acc, axis=-1, keepdims=True)
                acc = acc * lax.rsqrt(ms + EPS) * gf_ref[...][None]
        o_ref[...] = acc

    pl.when(j == 0)(lambda: step(True, False))
    pl.when(jnp.logical_and(j > 0, j < last))(lambda: step(False, False))
    pl.when(j == last)(lambda: step(False, True))


def _ffn(x3, g, mod, weights, layer, g_final, groups_per_tile, batch_of_tile, final_norm):
    ng, rn, d = x3.shape
    gt = groups_per_tile
    tm = gt * rn
    emit = len(weights) == 2
    x_spec = pl.BlockSpec((gt, rn, d), lambda i, j: (i, 0, 0))
    if emit:
        w_gu, w_down = weights
        dff = w_down.shape[1]
        tf = FFN_CAST_TILE
        nf = dff // tf
        w_specs = [
            pl.BlockSpec((None, d, tf), lambda i, j: (layer, 0, j)),
            pl.BlockSpec((None, d, tf), lambda i, j: (layer, 0, j + nf)),
            pl.BlockSpec((None, tf, d), lambda i, j: (layer, j, 0)),
        ]
        w_args = (w_gu, w_gu, w_down)
    else:
        dff = weights[2].shape[0]
        tf = FFN_TILE
        nf = dff // tf
        w_args = weights
    bf_specs = [
        pl.BlockSpec((d, tf), lambda i, j: (0, j)),
        pl.BlockSpec((d, tf), lambda i, j: (0, j)),
        pl.BlockSpec((tf, d), lambda i, j: (j, 0)),
    ]
    if not emit:
        w_specs = bf_specs
    out_specs = [x_spec]
    out_shape = [jax.ShapeDtypeStruct(x3.shape, F32)]
    if emit:
        out_specs += bf_specs
        out_shape += [jax.ShapeDtypeStruct((d, dff), BF16), jax.ShapeDtypeStruct((d, dff), BF16),
                      jax.ShapeDtypeStruct((dff, d), BF16)]
    outs = pl.pallas_call(
        functools.partial(_ffn_kernel, final_norm=final_norm, emit_bf16=emit),
        grid=(ng // gt, nf),
        in_specs=[
            x_spec,
            _resident((1, d)),
            pl.BlockSpec((gt, 6, d), lambda i, j: (batch_of_tile(i), 0, 0)),
            *w_specs,
            _resident((1, d)),
        ],
        out_specs=out_specs,
        out_shape=out_shape,
        scratch_shapes=[pltpu.VMEM((tm, d), BF16)],
        compiler_params=_params("arbitrary", "arbitrary"),
        name="ffn",
    )(x3, g, mod, *w_args, g_final)
    return outs[0], tuple(outs[1:])


def _odd_in_kernel(x_ref, g_ref, mod_ref, w_ref, o_ref, *rest,
                   start, tiles_per_seq, qk_cols, v_cols, head_dim, emit_bf16):
    i = pl.program_id(0)
    j = pl.program_id(1)
    h_ref, cos_ref, sin_ref = rest[-3:]
    tm = h_ref.shape[0]
    tn = o_ref.shape[1]
    half = head_dim // 2
    nq = qk_cols // tn
    nv = v_cols // tn

    def prologue():
        h_ref[...] = _norm_modulate(x_ref, g_ref, mod_ref, 0, 1).astype(BF16)
        gn, rn, _ = x_ref.shape
        row = lax.broadcasted_iota(jnp.int32, (gn, rn, half), 1).reshape(tm, half)
        pos = (row + (start + (i % tiles_per_seq) * rn)).astype(F32)
        lane = lax.broadcasted_iota(jnp.int32, (tm, half), 1).astype(F32)
        ang = pos * jnp.exp(lane * (-math.log(ROPE_BASE) / half))
        cos_ref[...] = jnp.cos(ang)
        sin_ref[...] = jnp.sin(ang)

    def project(epilogue):
        h = h_ref[...]
        for n in range(tn // PROJ_SUB):
            cols = slice(n * PROJ_SUB, (n + 1) * PROJ_SUB)
            w = w_ref[:, cols]
            if emit_bf16:
                w = w.astype(BF16)
                rest[0][:, cols] = w
            epilogue(_dot(h, w), n * PROJ_SUB)

    def rope(scale):
        def epilogue(acc, col0):
            cos = cos_ref[...]
            sin = sin_ref[...]
            for hd in range(PROJ_SUB // head_dim):
                lo = hd * head_dim
                x1 = acc[:, lo:lo + half]
                x2 = acc[:, lo + half:lo + head_dim]
                o_ref[:, col0 + lo:col0 + lo + half] = ((x1 * cos - x2 * sin) * scale).astype(BF16)
                o_ref[:, col0 + lo + half:col0 + lo + head_dim] = (
                    (x1 * sin + x2 * cos) * scale).astype(BF16)
        return epilogue

    def store(fn):
        def epilogue(acc, col0):
            o_ref[:, col0:col0 + PROJ_SUB] = fn(acc).astype(BF16)
        return epilogue

    @pl.when(j == 0)
    def _():
        prologue()
        project(rope(1.0))

    if nq > 1:
        @pl.when(jnp.logical_and(j > 0, j < nq))
        def _():
            project(rope(1.0))

    @pl.when(jnp.logical_and(j >= nq, j < 2 * nq))
    def _():
        project(rope(head_dim ** -0.5))

    @pl.when(jnp.logical_and(j >= 2 * nq, j < 2 * nq + nv))
    def _():
        project(store(lambda a: a))

    @pl.when(j >= 2 * nq + nv)
    def _():
        project(store(_silu))


def _odd_in(x3, g, mod, w_in, groups_per_tile, batch_of_tile, start, tiles_per_seq):
    ng, rn, d = x3.shape
    gt = groups_per_tile
    tm = gt * rn
    m = ng * rn
    n = w_in.shape[1]
    emit = w_in.dtype != BF16
    tn = PROJ_CAST_TILE if emit else PROJ_TILE
    head_dim = d // RET_HEADS
    kern = functools.partial(_odd_in_kernel, start=start, tiles_per_seq=tiles_per_seq, qk_cols=d,
                             v_cols=2 * d, head_dim=head_dim, emit_bf16=emit)
    w_spec = pl.BlockSpec((d, tn), lambda i, j: (0, j))
    out_specs = [pl.BlockSpec((tm, tn), lambda i, j: (i, j))]
    out_shape = [jax.ShapeDtypeStruct((m, n), BF16)]
    if emit:
        out_specs.append(w_spec)
        out_shape.append(jax.ShapeDtypeStruct((d, n), BF16))
    outs = pl.pallas_call(
        kern,
        grid=(ng // gt, n // tn),
        in_specs=[
            pl.BlockSpec((gt, rn, d), lambda i, j: (i, 0, 0)),
            _resident((1, d)),
            pl.BlockSpec((gt, 6, d), lambda i, j: (batch_of_tile(i), 0, 0)),
            w_spec,
        ],
        out_specs=out_specs,
        out_shape=out_shape,
        scratch_shapes=[pltpu.VMEM((tm, d), BF16), pltpu.VMEM((tm, head_dim // 2), F32),
                        pltpu.VMEM((tm, head_dim // 2), F32)],
        compiler_params=_params("arbitrary", "arbitrary"),
        name="odd_in",
    )(x3, g, mod, w_in)
    return outs[0], (outs[1] if emit else w_in)


def _retention_kernel(lg_ref, q_ref, k_ref, v_ref, g_ref, s0_ref, x_ref, mod_ref, w_ref,
                      o_ref, s_ref, dec_ref, dq_ref, dk_ref, a_ref, *, zero_state):
    c = pl.program_id(1)
    hds, dk_dim, dv_dim = s_ref.shape[1:]
    cl = q_ref.shape[0]
    lanes = dq_ref.shape[-1]

    @pl.when(c == 0)
    def _():
        if zero_state:
            s_ref[...] = jnp.zeros_like(s_ref)
        else:
            s_ref[...] = s0_ref[...]
        li = lax.broadcasted_iota(jnp.int32, (cl, cl), 0)
        mi = lax.broadcasted_iota(jnp.int32, (cl, cl), 1)
        diff = li - mi
        row = lax.broadcasted_iota(jnp.int32, (cl, lanes), 0).astype(F32)
        for hd in range(hds):
            log_g = lg_ref[hd]
            dec = jnp.exp(jnp.maximum(diff, 0).astype(F32) * log_g)
            dec_ref[hd] = jnp.where(diff >= 0, dec, 0.0)
            dq_ref[hd] = jnp.exp((row + 1.0) * log_g)
            dk_ref[hd] = jnp.exp((cl - 1.0 - row) * log_g)

    for hd in range(hds):
        log_g = lg_ref[hd]
        q = q_ref[:, hd * dk_dim:(hd + 1) * dk_dim]
        k = k_ref[:, hd * dk_dim:(hd + 1) * dk_dim]
        v = v_ref[:, hd * dv_dim:(hd + 1) * dv_dim]
        s_prev = s_ref[0, hd]
        dq = jnp.concatenate([dq_ref[hd]] * (dv_dim // lanes), axis=1)
        dk = jnp.concatenate([dk_ref[hd]] * (dk_dim // lanes), axis=1)
        scores = lax.dot_general(q, k, (((1,), (1,)), ((), ())), preferred_element_type=F32)
        p = (scores * dec_ref[hd]).astype(BF16)
        o = _dot(p, v) + _dot(q, s_prev.astype(BF16)) * dq
        k_dec = (k.astype(F32) * dk).astype(BF16)
        kv = lax.dot_general(k_dec, v, (((0,), (0,)), ((), ())), preferred_element_type=F32)
        s_ref[0, hd] = s_prev * jnp.exp(jnp.full((1, 1), cl, F32) * log_g) + kv
        o = o * lax.rsqrt(jnp.mean(o * o, axis=-1, keepdims=True) + EPS)
        gate = g_ref[:, hd * dv_dim:(hd + 1) * dv_dim].astype(F32)
        a_ref[:, hd * dv_dim:(hd + 1) * dv_dim] = (gate * o).astype(BF16)
        if (hd + 1) % RET_PROJ_HEADS == 0:
            rows = slice((hd + 1 - RET_PROJ_HEADS) * dv_dim, (hd + 1) * dv_dim)
            part = _dot(a_ref[:, rows], w_ref[rows, :])
            y = part if hd + 1 == RET_PROJ_HEADS else y + part

    o_ref[...] = x_ref[...] + mod_ref[:, 2:3, :] * y[None]


def _retention(proj, s0, x, mod, w_out, log_g, batch, seq_len, zero_state):
    m, d = x.shape
    hds = RET_HEADS
    dk_dim = d // hds
    dv_dim = 2 * d // hds
    cl = min(seq_len, RET_CHUNK)
    nc = seq_len // cl
    lanes = LANES
    state_spec = pl.BlockSpec((1, hds, dk_dim, dv_dim), lambda b, c: (b, 0, 0, 0))
    s0_spec = _resident(s0.shape) if zero_state else state_spec
    x_spec = pl.BlockSpec((1, cl, d), lambda b, c: (b * nc + c, 0, 0))
    y, s_fin = pl.pallas_call(
        functools.partial(_retention_kernel, zero_state=zero_state),
        grid=(batch, nc),
        in_specs=[
            pl.BlockSpec(memory_space=pltpu.SMEM),
            pl.BlockSpec((cl, d), lambda b, c: (b * nc + c, 0)),
            pl.BlockSpec((cl, d), lambda b, c: (b * nc + c, 1)),
            pl.BlockSpec((cl, 2 * d), lambda b, c: (b * nc + c, 1)),
            pl.BlockSpec((cl, 2 * d), lambda b, c: (b * nc + c, 2)),
            s0_spec,
            x_spec,
            pl.BlockSpec((1, 6, d), lambda b, c: (b, 0, 0)),
            _resident(w_out.shape),
        ],
        out_specs=[x_spec, state_spec],
        out_shape=[
            jax.ShapeDtypeStruct((m // cl, cl, d), F32),
            jax.ShapeDtypeStruct((batch, hds, dk_dim, dv_dim), F32),
        ],
        scratch_shapes=[
            pltpu.VMEM((hds, cl, cl), F32),
            pltpu.VMEM((hds, cl, lanes), F32),
            pltpu.VMEM((hds, cl, lanes), F32),
            pltpu.VMEM((cl, 2 * d), BF16),
        ],
        compiler_params=_params("arbitrary", "arbitrary"),
        name="retention",
    )(log_g, proj, proj, proj, proj, s0, x.reshape(m // cl, cl, d), mod, w_out)
    return y.reshape(m, d), s_fin


def _tiling(batch, seq_len, row_tile):
    tile = min(row_tile, batch * seq_len)
    if seq_len >= tile:
        tiles_per_seq = seq_len // tile
        return tile, 1, tiles_per_seq, lambda i: i // tiles_per_seq
    return seq_len, tile // seq_len, 1, lambda i: i


def _stream(x, mods, pool_hist, ret_state, start, prm, heavy, uv_dtype):
    heavy_bf = {"ffn": [None] * len(heavy["ffn"])}
    batch, seq_len, d = x.shape
    m = batch * seq_len
    rn, gt, tiles_per_seq, batch_of_tile = _tiling(batch, seq_len, ROW_TILE)
    tm = rn * gt
    x3 = x.reshape(m // rn, rn, d)
    seg = min(seq_len, tm)
    rn_f, gt_f, _, batch_of_ffn_tile = _tiling(batch, seq_len, FFN_ROW_TILE)

    def ffn(x3, layer, final_norm):
        y, emitted = _ffn(x3.reshape(m // rn_f, rn_f, d), prm["norm_ffn"][layer], mods[layer],
                          heavy["ffn"][layer], layer, prm["norm_final"], gt_f, batch_of_ffn_tile,
                          final_norm)
        heavy_bf["ffn"][layer] = emitted or heavy["ffn"][layer]
        return y.reshape(x3.shape)

    mod = mods[0]
    u, v, xb = _even_in(x3, prm["norm_mix"][0], mod, prm["w_in_ab"], prm["ln_v_g"], prm["ln_v_b"],
                        gt, batch_of_tile, uv_dtype)
    bw = xb.shape[1]
    xb3 = xb.reshape(m // seg, seg, bw)
    if pool_hist is None:
        hist = xb.reshape(m // POOL_HIST, POOL_HIST, bw)
        per_tile = tm // POOL_HIST
        hist_of_tile = lambda i: jnp.maximum(i * per_tile - 1, 0)
        zero_first = True
    else:
        hist = jnp.pad(pool_hist, ((0, 0), (POOL_HIST - POOL_PAD, 0), (0, 0)))
        hist_of_tile = lambda i: i
        zero_first = False
    x3 = _even_mix(u, v, xb3, hist, x3, mod, prm["ws"][seq_len], prm["bs_t"][seq_len],
                   prm["w_pool"], prm["pool_scale"], prm["w_out_ab"], gt, batch_of_tile,
                   hist_of_tile, start, tiles_per_seq, zero_first)
    x3 = ffn(x3, 0, False)
    pool_state = xb.reshape(batch, seq_len, bw)[:, seq_len - POOL_PAD:]
    v_state = v.reshape(batch, seq_len, -1)

    mod = mods[1]
    rn_o, gt_o, tiles_per_seq_o, batch_of_proj_tile = _tiling(batch, seq_len, PROJ_ROW_TILE)
    proj, heavy_bf["w_in_c"] = _odd_in(x3.reshape(m // rn_o, rn_o, d), prm["norm_mix"][1], mod,
                                       heavy["w_in_c"], gt_o, batch_of_proj_tile, start,
                                       tiles_per_seq_o)
    if ret_state is None:
        s0 = jnp.zeros((1, 1, d // RET_HEADS, 2 * d // RET_HEADS), F32)
    else:
        s0 = ret_state
    x2, s_fin = _retention(proj, s0, x3.reshape(m, d), mod, prm["w_out_c"], prm["log_g"], batch,
                           seq_len, ret_state is None)
    y3 = ffn(x2.reshape(x3.shape), 1, True)
    return (y3.reshape(batch, seq_len, d), pool_state, v_state, s_fin), heavy_bf


def _spatial_weights(w_s, b_s, seq_len):
    blk = min(seq_len, A_BLOCK)
    rep = A_BLOCK // blk
    idx = jnp.arange(A_BLOCK)
    t, s = idx[:, None], idx[None, :]
    same_seq = (t // blk) == (s // blk)
    causal = ((s % blk) // CHUNK) <= ((t % blk) // CHUNK)
    corner = jnp.tile(w_s[:, :blk, :blk], (1, rep, rep))
    ws = jnp.where((same_seq & causal)[None], corner, 0.0).astype(BF16)
    bs_t = jnp.tile(b_s[:, :blk], (1, rep)).T
    return ws, bs_t


def kernel(x_prompt, x_sample, c_prompt, c_sample, state_b_pool, state_c_ret, w_ada, b_ada, norm_mix, norm_ffn, norm_final, w_in_ab, ln_v_g, ln_v_b, w_s, b_s, w_pool, pool_scale, w_out_ab, w_in_c, w_out_c, w_ffn_gu, w_ffn_down):
    bp, sp, d = x_prompt.shape
    bs, ss, _ = x_sample.shape
    depth = w_ada.shape[0]
    past_len = 2048

    rows = bp + bs
    pad_rows = -rows % 16
    c_all = jnp.concatenate([c_prompt, c_sample, jnp.zeros((pad_rows, d), F32)], axis=0)
    ada = _ada(c_all, w_ada, b_ada).reshape(depth, rows + pad_rows, 6, d)
    mods_p = [ada[l, :bp] for l in range(depth)]
    mods_s = [ada[l, bp:rows] for l in range(depth)]

    log_g = jnp.log1p(-jnp.exp2(-5.0 - jnp.arange(RET_HEADS, dtype=F32)))
    prm = {
        "norm_mix": norm_mix.reshape(depth, 1, d),
        "norm_ffn": norm_ffn.reshape(depth, 1, d),
        "norm_final": norm_final.reshape(1, d),
        "w_in_ab": w_in_ab[0].astype(BF16),
        "ln_v_g": ln_v_g[0].reshape(1, -1),
        "ln_v_b": ln_v_b[0].reshape(1, -1),
        "ws": {}, "bs_t": {},
        "w_pool": w_pool[0].astype(BF16),
        "pool_scale": pool_scale[0].reshape(1, -1),
        "w_out_ab": w_out_ab[0].astype(BF16),
        "w_out_c": w_out_c[0].astype(BF16),
        "log_g": log_g,
    }
    for sl in (sp, ss):
        prm["ws"][sl], prm["bs_t"][sl] = _spatial_weights(w_s[0], b_s[0], sl)

    heavy = {"ffn": [(w_ffn_gu, w_ffn_down)] * depth, "w_in_c": w_in_c[0]}
    (y_s, pool_s, v_s, ret_s), heavy_bf = _stream(x_sample, mods_s, state_b_pool[0], state_c_ret[0],
                                                  past_len, prm, heavy, F32)
    (y_p, pool_p, _, ret_p), _ = _stream(x_prompt, mods_p, None, None, 0, prm, heavy_bf, BF16)
    return (y_p, y_s, pool_p[None], pool_s[None], v_s[None], ret_p[None], ret_s[None])
```

```python
import functools
import math

import jax
import jax.numpy as jnp
from jax import lax
from jax.experimental import pallas as pl
from jax.experimental.pallas import tpu as pltpu

EPS = 1e-6
CHUNK = 64
A_BLOCK = 128
A_GROUPS = 8
POOL_WINDOWS = (2, 4, 8, 16)
POOL_PAD = 15
POOL_HIST = 16
RET_HEADS = 8
ROPE_BASE = 10000.0
LANES = 128

F32 = jnp.float32
BF16 = jnp.bfloat16

VMEM_LIMIT_BYTES = 56 * 1024 * 1024
ROW_TILE = 512
FFN_ROW_TILE = 1024
PROJ_ROW_TILE = 1024
FFN_TILE = 512
PROJ_TILE = 2048
PROJ_SUB = 512
FFN_CAST_TILE = 256
PROJ_CAST_TILE = 1024
RET_CHUNK = 256
RET_PROJ_HEADS = 4


def _params(*sem):
    return pltpu.CompilerParams(dimension_semantics=sem, vmem_limit_bytes=VMEM_LIMIT_BYTES)


def _resident(shape):
    nd = len(shape)
    return pl.BlockSpec(shape, lambda *_: (0,) * nd, pipeline_mode=pl.Buffered(1))


def _dot(a, b):
    return jnp.dot(a, b, preferred_element_type=F32)


def _silu(x):
    return x * jax.nn.sigmoid(x)


def _norm_modulate(x_ref, g_ref, mod_ref, i_shift, i_scale):
    x = x_ref[...]
    gn, rn, dn = x.shape
    ms = jnp.mean(x * x, axis=-1, keepdims=True)
    y = x * lax.rsqrt(ms + EPS) * g_ref[...][None]
    shift = mod_ref[:, i_shift:i_shift + 1, :]
    scale = mod_ref[:, i_scale:i_scale + 1, :]
    h = y * (1.0 + scale) + shift
    return h.reshape(gn * rn, dn)


def _ada_kernel(c_ref, w_ref, b_ref, o_ref):
    s = _silu(c_ref[...]).astype(BF16)
    o_ref[0] = _dot(s, w_ref[0].astype(BF16)) + b_ref[0]


def _ada(c_all, w_ada, b_ada):
    depth, d, n = w_ada.shape
    rows = c_all.shape[0]
    tn = 1024
    return pl.pallas_call(
        _ada_kernel,
        grid=(depth, n // tn),
        in_specs=[
            pl.BlockSpec((rows, d), lambda l, j: (0, 0)),
            pl.BlockSpec((1, d, tn), lambda l, j: (l, 0, j)),
            pl.BlockSpec((1, 1, tn), lambda l, j: (l, 0, j)),
        ],
        out_specs=pl.BlockSpec((1, rows, tn), lambda l, j: (l, 0, j)),
        out_shape=jax.ShapeDtypeStruct((depth, rows, n), F32),
        compiler_params=_params("arbitrary", "arbitrary"),
        name="ada",
    )(c_all, w_ada, b_ada.reshape(depth, 1, n))


def _even_kernel(x_ref, g_ref, mod_ref, w_ref, lng_ref, lnb_ref, hist_ref, ws_ref, bs_ref, wp_ref,
                 ps_ref, wo_ref, o_ref, v_ref, xb_ref, u_ref, y_ref, carry_ref,
                 *, start, tiles_per_seq, seg, carry_hist):
    i = pl.program_id(0)
    tm, aw = u_ref.shape
    bw = xb_ref.shape[1]
    nseg = tm // seg
    nblk = tm // A_BLOCK
    gdim = aw // A_GROUPS

    h = _norm_modulate(x_ref, g_ref, mod_ref, 0, 1).astype(BF16)
    u_ref[...] = jax.nn.gelu(_dot(h, w_ref[:, 0:aw])).astype(u_ref.dtype)
    v = jax.nn.gelu(_dot(h, w_ref[:, aw:2 * aw]))
    vc = v - jnp.mean(v, axis=-1, keepdims=True)
    vn = vc * lax.rsqrt(jnp.mean(vc * vc, axis=-1, keepdims=True) + EPS)
    v_ref[...] = (vn * lng_ref[...] + lnb_ref[...]).astype(v_ref.dtype)
    xb_ref[...] = _dot(h, w_ref[:, 2 * aw:3 * aw])

    for g in range(A_GROUPS):
        cols = slice(g * gdim, (g + 1) * gdim)
        vg = jnp.concatenate(
            [v_ref[n * A_BLOCK:(n + 1) * A_BLOCK, cols].astype(BF16) for n in range(nblk)], axis=1)
        mixed = _dot(ws_ref[g], vg) + bs_ref[:, g:g + 1]
        for n in range(nblk):
            rows = slice(n * A_BLOCK, (n + 1) * A_BLOCK)
            piece = mixed[:, n * gdim:(n + 1) * gdim]
            y_ref[rows, cols] = (u_ref[rows, cols].astype(F32) * piece).astype(BF16)

    xb = xb_ref[...].reshape(nseg, seg, bw)
    if carry_hist:
        @pl.when(i == 0)
        def _():
            carry_ref[...] = jnp.zeros_like(carry_ref)

        hist = jnp.where(i % tiles_per_seq == 0, 0.0, carry_ref[...])
        carry_ref[...] = xb[:, seg - POOL_HIST:, :]
    else:
        hist = hist_ref[...]
    ext = jnp.concatenate([hist, xb], axis=1)
    off = start + (i % tiles_per_seq) * seg if tiles_per_seq > 1 else start
    posp1 = lax.broadcasted_iota(jnp.int32, (1, seg, 1), 1) + (off + 1)
    pdim = bw // len(POOL_WINDOWS)
    for g, w in enumerate(POOL_WINDOWS):
        cols = slice(g * pdim, (g + 1) * pdim)
        a = ext[:, :, cols]
        tot = a
        k = 1
        while k < w:
            tot = tot + pltpu.roll(tot, k, axis=1)
            k *= 2
        inv = 1.0 / jnp.minimum(posp1, w).astype(F32)
        pooled = tot[:, POOL_HIST:, :] * inv - a[:, POOL_HIST:, :]
        pooled = pooled.reshape(tm, pdim).astype(BF16)
        yb = _dot(pooled, wp_ref[g]) * ps_ref[:, cols]
        y_ref[:, aw + g * pdim:aw + (g + 1) * pdim] = yb.astype(BF16)

    out = _dot(y_ref[...], wo_ref[...])
    x = x_ref[...]
    o_ref[...] = x + mod_ref[:, 2:3, :] * out.reshape(x.shape)


def _even(x3, g, mod, w_in, ln_g, ln_b, hist, ws, bs_t, w_pool, pool_scale, w_out,
          groups_per_tile, batch_of_tile, start, tiles_per_seq, seg, carry_hist, uv_dtype):
    ng, rn, d = x3.shape
    gt = groups_per_tile
    tm = gt * rn
    m = ng * rn
    aw = w_in.shape[1] // 3
    bw = aw
    nseg = tm // seg
    row = lambda i: (i, 0)
    tile = pl.BlockSpec((gt, rn, d), lambda i: (i, 0, 0))
    kern = functools.partial(_even_kernel, start=start, tiles_per_seq=tiles_per_seq, seg=seg,
                             carry_hist=carry_hist)
    return pl.pallas_call(
        kern,
        grid=(ng // gt,),
        in_specs=[
            tile,
            _resident((1, d)),
            pl.BlockSpec((gt, 6, d), lambda i: (batch_of_tile(i), 0, 0)),
            _resident(w_in.shape),
            _resident((1, aw)),
            _resident((1, aw)),
            _resident(hist.shape),
            _resident(ws.shape),
            _resident(bs_t.shape),
            _resident(w_pool.shape),
            _resident(pool_scale.shape),
            _resident(w_out.shape),
        ],
        out_specs=[tile, pl.BlockSpec((tm, aw), row), pl.BlockSpec((tm, bw), row)],
        out_shape=[
            jax.ShapeDtypeStruct(x3.shape, F32),
            jax.ShapeDtypeStruct((m, aw), uv_dtype),
            jax.ShapeDtypeStruct((m, bw), F32),
        ],
        scratch_shapes=[pltpu.VMEM((tm, aw), uv_dtype), pltpu.VMEM((tm, aw + bw), BF16),
                        pltpu.VMEM((nseg, POOL_HIST, bw), F32)],
        compiler_params=_params("arbitrary"),
        name="even",
    )(x3, g, mod, w_in, ln_g, ln_b, hist, ws, bs_t, w_pool, pool_scale, w_out)


def _ffn_kernel(x_ref, g_ref, mod_ref, wg_ref, wu_ref, wd_ref, gf_ref, o_ref, *rest,
                final_norm, emit_bf16):
    j = pl.program_id(1)
    last = pl.num_programs(1) - 1
    h_ref = rest[-1]

    def weights():
        if not emit_bf16:
            return wg_ref[...], wu_ref[...], wd_ref[...]
        tiles = tuple(r[...].astype(BF16) for r in (wg_ref, wu_ref, wd_ref))
        for out_ref, tile in zip(rest[:3], tiles):
            out_ref[...] = tile
        return tiles

    def step(first, final):
        if first:
            h = _norm_modulate(x_ref, g_ref, mod_ref, 3, 4).astype(BF16)
            h_ref[...] = h
        else:
            h = h_ref[...]
        wg, wu, wd = weights()
        a = (_silu(_dot(h, wg)) * _dot(h, wu)).astype(BF16)
        y = _dot(a, wd).reshape(o_ref.shape)
        acc = y if first else o_ref[...] + y
        if final:
            acc = x_ref[...] + mod_ref[:, 5:6, :] * acc
            if final_norm:
                ms = jnp.mean(acc * acc, axis=-1, keepdims=True)
                acc = acc * lax.rsqrt(ms + EPS) * gf_ref[...][None]
        o_ref[...] = acc

    pl.when(j == 0)(lambda: step(True, False))
    pl.when(jnp.logical_and(j > 0, j < last))(lambda: step(False, False))
    pl.when(j == last)(lambda: step(False, True))


def _ffn(x3, g, mod, weights, layer, g_final, groups_per_tile, batch_of_tile, final_norm):
    ng, rn, d = x3.shape
    gt = groups_per_tile
    tm = gt * rn
    emit = len(weights) == 2
    x_spec = pl.BlockSpec((gt, rn, d), lambda i, j: (i, 0, 0))
    if emit:
        w_gu, w_down = weights
        dff = w_down.shape[1]
        tf = FFN_CAST_TILE
        nf = dff // tf
        w_specs = [
            pl.BlockSpec((None, d, tf), lambda i, j: (layer, 0, j)),
            pl.BlockSpec((None, d, tf), lambda i, j: (layer, 0, j + nf)),
            pl.BlockSpec((None, tf, d), lambda i, j: (layer, j, 0)),
        ]
        w_args = (w_gu, w_gu, w_down)
    else:
        dff = weights[2].shape[0]
        tf = FFN_TILE
        nf = dff // tf
        w_args = weights
    bf_specs = [
        pl.BlockSpec((d, tf), lambda i, j: (0, j)),
        pl.BlockSpec((d, tf), lambda i, j: (0, j)),
        pl.BlockSpec((tf, d), lambda i, j: (j, 0)),
    ]
    if not emit:
        w_specs = bf_specs
    out_specs = [x_spec]
    out_shape = [jax.ShapeDtypeStruct(x3.shape, F32)]
    if emit:
        out_specs += bf_specs
        out_shape += [jax.ShapeDtypeStruct((d, dff), BF16), jax.ShapeDtypeStruct((d, dff), BF16),
                      jax.ShapeDtypeStruct((dff, d), BF16)]
    outs = pl.pallas_call(
        functools.partial(_ffn_kernel, final_norm=final_norm, emit_bf16=emit),
        grid=(ng // gt, nf),
        in_specs=[
            x_spec,
            _resident((1, d)),
            pl.BlockSpec((gt, 6, d), lambda i, j: (batch_of_tile(i), 0, 0)),
            *w_specs,
            _resident((1, d)),
        ],
        out_specs=out_specs,
        out_shape=out_shape,
        scratch_shapes=[pltpu.VMEM((tm, d), BF16)],
        compiler_params=_params("arbitrary", "arbitrary"),
        name="ffn",
    )(x3, g, mod, *w_args, g_final)
    return outs[0], tuple(outs[1:])


def _odd_in_kernel(x_ref, g_ref, mod_ref, w_ref, o_ref, *rest,
                   start, tiles_per_seq, qk_cols, v_cols, head_dim, emit_bf16):
    i = pl.program_id(0)
    j = pl.program_id(1)
    h_ref, cos_ref, sin_ref = rest[-3:]
    tm = h_ref.shape[0]
    tn = o_ref.shape[1]
    half = head_dim // 2
    nq = qk_cols // tn
    nv = v_cols // tn

    def prologue():
        h_ref[...] = _norm_modulate(x_ref, g_ref, mod_ref, 0, 1).astype(BF16)
        gn, rn, _ = x_ref.shape
        row = lax.broadcasted_iota(jnp.int32, (gn, rn, half), 1).reshape(tm, half)
        pos = (row + (start + (i % tiles_per_seq) * rn)).astype(F32)
        lane = lax.broadcasted_iota(jnp.int32, (tm, half), 1).astype(F32)
        ang = pos * jnp.exp(lane * (-math.log(ROPE_BASE) / half))
        cos_ref[...] = jnp.cos(ang)
        sin_ref[...] = jnp.sin(ang)

    def project(epilogue):
        h = h_ref[...]
        for n in range(tn // PROJ_SUB):
            cols = slice(n * PROJ_SUB, (n + 1) * PROJ_SUB)
            w = w_ref[:, cols]
            if emit_bf16:
                w = w.astype(BF16)
                rest[0][:, cols] = w
            epilogue(_dot(h, w), n * PROJ_SUB)

    def rope(scale):
        def epilogue(acc, col0):
            cos = cos_ref[...]
            sin = sin_ref[...]
            for hd in range(PROJ_SUB // head_dim):
                lo = hd * head_dim
                x1 = acc[:, lo:lo + half]
                x2 = acc[:, lo + half:lo + head_dim]
                o_ref[:, col0 + lo:col0 + lo + half] = ((x1 * cos - x2 * sin) * scale).astype(BF16)
                o_ref[:, col0 + lo + half:col0 + lo + head_dim] = (
                    (x1 * sin + x2 * cos) * scale).astype(BF16)
        return epilogue

    def store(fn):
        def epilogue(acc, col0):
            o_ref[:, col0:col0 + PROJ_SUB] = fn(acc).astype(BF16)
        return epilogue

    @pl.when(j == 0)
    def _():
        prologue()
        project(rope(1.0))

    if nq > 1:
        @pl.when(jnp.logical_and(j > 0, j < nq))
        def _():
            project(rope(1.0))

    @pl.when(jnp.logical_and(j >= nq, j < 2 * nq))
    def _():
        project(rope(head_dim ** -0.5))

    @pl.when(jnp.logical_and(j >= 2 * nq, j < 2 * nq + nv))
    def _():
        project(store(lambda a: a))

    @pl.when(j >= 2 * nq + nv)
    def _():
        project(store(_silu))


def _odd_in(x3, g, mod, w_in, groups_per_tile, batch_of_tile, start, tiles_per_seq):
    ng, rn, d = x3.shape
    gt = groups_per_tile
    tm = gt * rn
    m = ng * rn
    n = w_in.shape[1]
    emit = w_in.dtype != BF16
    tn = PROJ_CAST_TILE if emit else PROJ_TILE
    head_dim = d // RET_HEADS
    kern = functools.partial(_odd_in_kernel, start=start, tiles_per_seq=tiles_per_seq, qk_cols=d,
                             v_cols=2 * d, head_dim=head_dim, emit_bf16=emit)
    w_spec = pl.BlockSpec((d, tn), lambda i, j: (0, j))
    out_specs = [pl.BlockSpec((tm, tn), lambda i, j: (i, j))]
    out_shape = [jax.ShapeDtypeStruct((m, n), BF16)]
    if emit:
        out_specs.append(w_spec)
        out_shape.append(jax.ShapeDtypeStruct((d, n), BF16))
    outs = pl.pallas_call(
        kern,
        grid=(ng // gt, n // tn),
        in_specs=[
            pl.BlockSpec((gt, rn, d), lambda i, j: (i, 0, 0)),
            _resident((1, d)),
            pl.BlockSpec((gt, 6, d), lambda i, j: (batch_of_tile(i), 0, 0)),
            w_spec,
        ],
        out_specs=out_specs,
        out_shape=out_shape,
        scratch_shapes=[pltpu.VMEM((tm, d), BF16), pltpu.VMEM((tm, head_dim // 2), F32),
                        pltpu.VMEM((tm, head_dim // 2), F32)],
        compiler_params=_params("arbitrary", "arbitrary"),
        name="odd_in",
    )(x3, g, mod, w_in)
    return outs[0], (outs[1] if emit else w_in)


def _retention_kernel(lg_ref, q_ref, k_ref, v_ref, g_ref, s0_ref, x_ref, mod_ref, w_ref,
                      o_ref, s_ref, dec_ref, dq_ref, dk_ref, a_ref, *, zero_state):
    c = pl.program_id(1)
    hds, dk_dim, dv_dim = s_ref.shape[1:]
    cl = q_ref.shape[0]
    lanes = dq_ref.shape[-1]

    @pl.when(c == 0)
    def _():
        if zero_state:
            s_ref[...] = jnp.zeros_like(s_ref)
        else:
            s_ref[...] = s0_ref[...]
        li = lax.broadcasted_iota(jnp.int32, (cl, cl), 0)
        mi = lax.broadcasted_iota(jnp.int32, (cl, cl), 1)
        diff = li - mi
        row = lax.broadcasted_iota(jnp.int32, (cl, lanes), 0).astype(F32)
        for hd in range(hds):
            log_g = lg_ref[hd]
            dec = jnp.exp(jnp.maximum(diff, 0).astype(F32) * log_g)
            dec_ref[hd] = jnp.where(diff >= 0, dec, 0.0)
            dq_ref[hd] = jnp.exp((row + 1.0) * log_g)
            dk_ref[hd] = jnp.exp((cl - 1.0 - row) * log_g)

    for hd in range(hds):
        log_g = lg_ref[hd]
        q = q_ref[:, hd * dk_dim:(hd + 1) * dk_dim]
        k = k_ref[:, hd * dk_dim:(hd + 1) * dk_dim]
        v = v_ref[:, hd * dv_dim:(hd + 1) * dv_dim]
        s_prev = s_ref[0, hd]
        dq = jnp.concatenate([dq_ref[hd]] * (dv_dim // lanes), axis=1)
        dk = jnp.concatenate([dk_ref[hd]] * (dk_dim // lanes), axis=1)
        scores = lax.dot_general(q, k, (((1,), (1,)), ((), ())), preferred_element_type=F32)
        p = (scores * dec_ref[hd]).astype(BF16)
        o = _dot(p, v) + _dot(q, s_prev.astype(BF16)) * dq
        k_dec = (k.astype(F32) * dk).astype(BF16)
        kv = lax.dot_general(k_dec, v, (((0,), (0,)), ((), ())), preferred_element_type=F32)
        s_ref[0, hd] = s_prev * jnp.exp(jnp.full((1, 1), cl, F32) * log_g) + kv
        o = o * lax.rsqrt(jnp.mean(o * o, axis=-1, keepdims=True) + EPS)
        gate = g_ref[:, hd * dv_dim:(hd + 1) * dv_dim].astype(F32)
        a_ref[:, hd * dv_dim:(hd + 1) * dv_dim] = (gate * o).astype(BF16)
        if (hd + 1) % RET_PROJ_HEADS == 0:
            rows = slice((hd + 1 - RET_PROJ_HEADS) * dv_dim, (hd + 1) * dv_dim)
            part = _dot(a_ref[:, rows], w_ref[rows, :])
            y = part if hd + 1 == RET_PROJ_HEADS else y + part

    o_ref[...] = x_ref[...] + mod_ref[:, 2:3, :] * y[None]


def _retention(proj, s0, x, mod, w_out, log_g, batch, seq_len, zero_state):
    m, d = x.shape
    hds = RET_HEADS
    dk_dim = d // hds
    dv_dim = 2 * d // hds
    cl = min(seq_len, RET_CHUNK)
    nc = seq_len // cl
    lanes = LANES
    state_spec = pl.BlockSpec((1, hds, dk_dim, dv_dim), lambda b, c: (b, 0, 0, 0))
    s0_spec = _resident(s0.shape) if zero_state else state_spec
    x_spec = pl.BlockSpec((1, cl, d), lambda b, c: (b * nc + c, 0, 0))
    y, s_fin = pl.pallas_call(
        functools.partial(_retention_kernel, zero_state=zero_state),
        grid=(batch, nc),
        in_specs=[
            pl.BlockSpec(memory_space=pltpu.SMEM),
            pl.BlockSpec((cl, d), lambda b, c: (b * nc + c, 0)),
            pl.BlockSpec((cl, d), lambda b, c: (b * nc + c, 1)),
            pl.BlockSpec((cl, 2 * d), lambda b, c: (b * nc + c, 1)),
            pl.BlockSpec((cl, 2 * d), lambda b, c: (b * nc + c, 2)),
            s0_spec,
            x_spec,
            pl.BlockSpec((1, 6, d), lambda b, c: (b, 0, 0)),
            _resident(w_out.shape),
        ],
        out_specs=[x_spec, state_spec],
        out_shape=[
            jax.ShapeDtypeStruct((m // cl, cl, d), F32),
            jax.ShapeDtypeStruct((batch, hds, dk_dim, dv_dim), F32),
        ],
        scratch_shapes=[
            pltpu.VMEM((hds, cl, cl), F32),
            pltpu.VMEM((hds, cl, lanes), F32),
            pltpu.VMEM((hds, cl, lanes), F32),
            pltpu.VMEM((cl, 2 * d), BF16),
        ],
        compiler_params=_params("arbitrary", "arbitrary"),
        name="retention",
    )(log_g, proj, proj, proj, proj, s0, x.reshape(m // cl, cl, d), mod, w_out)
    return y.reshape(m, d), s_fin


def _tiling(batch, seq_len, row_tile):
    tile = min(row_tile, batch * seq_len)
    if seq_len >= tile:
        tiles_per_seq = seq_len // tile
        return tile, 1, tiles_per_seq, lambda i: i // tiles_per_seq
    return seq_len, tile // seq_len, 1, lambda i: i


def _stream(x, mods, pool_hist, ret_state, start, prm, heavy, uv_dtype):
    heavy_bf = {"ffn": [None] * len(heavy["ffn"])}
    batch, seq_len, d = x.shape
    m = batch * seq_len
    rn, gt, tiles_per_seq, batch_of_tile = _tiling(batch, seq_len, ROW_TILE)
    tm = rn * gt
    x3 = x.reshape(m // rn, rn, d)
    seg = min(seq_len, tm)
    rn_f, gt_f, _, batch_of_ffn_tile = _tiling(batch, seq_len, FFN_ROW_TILE)

    def ffn(x3, layer, final_norm):
        y, emitted = _ffn(x3.reshape(m // rn_f, rn_f, d), prm["norm_ffn"][layer], mods[layer],
                          heavy["ffn"][layer], layer, prm["norm_final"], gt_f, batch_of_ffn_tile,
                          final_norm)
        heavy_bf["ffn"][layer] = emitted or heavy["ffn"][layer]
        return y.reshape(x3.shape)

    mod = mods[0]
    bw = prm["w_in_ab"].shape[1] // 3
    if pool_hist is None:
        hist = jnp.zeros((1, POOL_HIST, bw), F32)
    else:
        hist = jnp.pad(pool_hist, ((0, 0), (POOL_HIST - POOL_PAD, 0), (0, 0)))
    x3, v, xb = _even(x3, prm["norm_mix"][0], mod, prm["w_in_ab"], prm["ln_v_g"], prm["ln_v_b"],
                      hist, prm["ws"][seq_len], prm["bs_t"][seq_len], prm["w_pool"],
                      prm["pool_scale"], prm["w_out_ab"], gt, batch_of_tile, start, tiles_per_seq,
                      seg, pool_hist is None, uv_dtype)
    x3 = ffn(x3, 0, False)
    pool_state = xb.reshape(batch, seq_len, bw)[:, seq_len - POOL_PAD:]
    v_state = v.reshape(batch, seq_len, -1)

    mod = mods[1]
    rn_o, gt_o, tiles_per_seq_o, batch_of_proj_tile = _tiling(batch, seq_len, PROJ_ROW_TILE)
    proj, heavy_bf["w_in_c"] = _odd_in(x3.reshape(m // rn_o, rn_o, d), prm["norm_mix"][1], mod,
                                       heavy["w_in_c"], gt_o, batch_of_proj_tile, start,
                                       tiles_per_seq_o)
    if ret_state is None:
        s0 = jnp.zeros((1, 1, d // RET_HEADS, 2 * d // RET_HEADS), F32)
    else:
        s0 = ret_state
    x2, s_fin = _retention(proj, s0, x3.reshape(m, d), mod, prm["w_out_c"], prm["log_g"], batch,
                           seq_len, ret_state is None)
    y3 = ffn(x2.reshape(x3.shape), 1, True)
    return (y3.reshape(batch, seq_len, d), pool_state, v_state, s_fin), heavy_bf


def _spatial_weights(w_s, b_s, seq_len):
    blk = min(seq_len, A_BLOCK)
    rep = A_BLOCK // blk
    idx = jnp.arange(A_BLOCK)
    t, s = idx[:, None], idx[None, :]
    same_seq = (t // blk) == (s // blk)
    causal = ((s % blk) // CHUNK) <= ((t % blk) // CHUNK)
    corner = jnp.tile(w_s[:, :blk, :blk], (1, rep, rep))
    ws = jnp.where((same_seq & causal)[None], corner, 0.0).astype(BF16)
    bs_t = jnp.tile(b_s[:, :blk], (1, rep)).T
    return ws, bs_t


def kernel(x_prompt, x_sample, c_prompt, c_sample, state_b_pool, state_c_ret, w_ada, b_ada, norm_mix, norm_ffn, norm_final, w_in_ab, ln_v_g, ln_v_b, w_s, b_s, w_pool, pool_scale, w_out_ab, w_in_c, w_out_c, w_ffn_gu, w_ffn_down):
    bp, sp, d = x_prompt.shape
    bs, ss, _ = x_sample.shape
    depth = w_ada.shape[0]
    past_len = 2048

    rows = bp + bs
    pad_rows = -rows % 16
    c_all = jnp.concatenate([c_prompt, c_sample, jnp.zeros((pad_rows, d), F32)], axis=0)
    ada = _ada(c_all, w_ada, b_ada).reshape(depth, rows + pad_rows, 6, d)
    mods_p = [ada[l, :bp] for l in range(depth)]
    mods_s = [ada[l, bp:rows] for l in range(depth)]

    log_g = jnp.log1p(-jnp.exp2(-5.0 - jnp.arange(RET_HEADS, dtype=F32)))
    prm = {
        "norm_mix": norm_mix.reshape(depth, 1, d),
        "norm_ffn": norm_ffn.reshape(depth, 1, d),
        "norm_final": norm_final.reshape(1, d),
        "w_in_ab": w_in_ab[0].astype(BF16),
        "ln_v_g": ln_v_g[0].reshape(1, -1),
        "ln_v_b": ln_v_b[0].reshape(1, -1),
        "ws": {}, "bs_t": {},
        "w_pool": w_pool[0].astype(BF16),
        "pool_scale": pool_scale[0].reshape(1, -1),
        "w_out_ab": w_out_ab[0].astype(BF16),
        "w_out_c": w_out_c[0].astype(BF16),
        "log_g": log_g,
    }
    for sl in (sp, ss):
        prm["ws"][sl], prm["bs_t"][sl] = _spatial_weights(w_s[0], b_s[0], sl)

    heavy = {"ffn": [(w_ffn_gu, w_ffn_down)] * depth, "w_in_c": w_in_c[0]}
    (y_s, pool_s, v_s, ret_s), heavy_bf = _stream(x_sample, mods_s, state_b_pool[0], state_c_ret[0],
                                                  past_len, prm, heavy, F32)
    (y_p, pool_p, _, ret_p), _ = _stream(x_prompt, mods_p, None, None, 0, prm, heavy_bf, BF16)
    return (y_p, y_s, pool_p[None], pool_s[None], v_s[None], ret_p[None], ret_s[None])
```
